```python
import math
import jax, jax.numpy as jnp
from jax import lax
import numpy as np

D_MODEL = 1024
BATCH = 4
SEQ = 8192
DEPTH = 1

MEM_LEN = 256
BRANCH_WIDTH = D_MODEL // 2
POOL_WINDOWS = (2, 4, 8, 16)
POOL_GROUPS = len(POOL_WINDOWS)
POOL_GROUP_DIM = BRANCH_WIDTH // POOL_GROUPS
DIFF_HEADS = 8
DIFF_QK_DIM = D_MODEL // 32
DIFF_V_DIM = 2 * DIFF_QK_DIM
DIFF_QK_WIDTH = DIFF_HEADS * 2 * DIFF_QK_DIM
MEM_HEADS = 4
MEM_HEAD_DIM = BRANCH_WIDTH // MEM_HEADS
N_BRANCH = 3
Q_BLOCK = 128
LN_EPS = 1e-5
RMS_EPS = 1e-5
DEEPNORM_ALPHA = (2.0 * DEPTH) ** 0.25
DEEPNORM_BETA = (8.0 * DEPTH) ** -0.25
IN_WIDTHS = (BRANCH_WIDTH, BRANCH_WIDTH,
             DIFF_QK_WIDTH, DIFF_QK_WIDTH,
             BRANCH_WIDTH, BRANCH_WIDTH,
             BRANCH_WIDTH, BRANCH_WIDTH,
             N_BRANCH * D_MODEL)
IN_TOTAL = sum(IN_WIDTHS)

kernel_name = "hybrid_pool_diffattn_memxattn_deepnorm"


def layer_norm(x, g, b):
    xf = x.astype(jnp.float32)
    mu = jnp.mean(xf, axis=-1, keepdims=True)
    var = jnp.mean(jnp.square(xf - mu), axis=-1, keepdims=True)
    y = (xf - mu) * lax.rsqrt(var + LN_EPS) * g.astype(jnp.float32) + b.astype(jnp.float32)
    return y.astype(x.dtype)


def rms_norm(x, g):
    xf = x.astype(jnp.float32)
    y = xf * lax.rsqrt(jnp.mean(jnp.square(xf), axis=-1, keepdims=True) + RMS_EPS)
    return (y * g.astype(jnp.float32)).astype(x.dtype)


def lambda_init_fn(layer_idx):
    return 0.8 - 0.6 * math.exp(-0.3 * layer_idx)


def alibi_slopes(n_heads):
    return jnp.array([2.0 ** (-8.0 * (h + 1) / n_heads) for h in range(n_heads)], dtype=jnp.float32)


def causal_multiscale_pool(u):
    b, s, _, c = u.shape
    t_count = jnp.arange(1, s + 1, dtype=jnp.float32)
    outs = []
    for g, w in enumerate(POOL_WINDOWS):
        ug = u[:, :, g, :].astype(jnp.float32)
        csum = jnp.concatenate([jnp.zeros((b, 1, c), jnp.float32), jnp.cumsum(ug, axis=1)], axis=1)
        upper = csum[:, 1:]
        lower = jnp.concatenate([jnp.zeros((b, w - 1, c), jnp.float32), csum[:, : s - w + 1]], axis=1)
        count = jnp.minimum(t_count, float(w))[None, :, None]
        outs.append((upper - lower) / count - ug)
    return jnp.stack(outs, axis=2).astype(u.dtype)


def differential_attention(q, k, v, lam):
    b, s, h, _, dq = q.shape
    n_blk = s // Q_BLOCK
    scale = dq ** -0.5
    slopes = alibi_slopes(h)
    k_pos = jnp.arange(s)
    q_blocks = q.reshape(b, n_blk, Q_BLOCK, h, 2, dq).transpose(1, 0, 2, 3, 4, 5)

    def one_block(args):
        q_blk, i = args
        q_pos = i * Q_BLOCK + jnp.arange(Q_BLOCK)
        sc = jnp.einsum('bqhcd,bkhcd->bhcqk', q_blk, k,
                        preferred_element_type=jnp.float32) * scale
        dist = q_pos[:, None] - k_pos[None, :]
        bias = -slopes[:, None, None] * dist.astype(jnp.float32)[None]
        sc = jnp.where((dist >= 0)[None, None, None], sc + bias[None, :, None], -jnp.inf)
        p = jax.nn.softmax(sc, axis=-1)
        a = p[:, :, 0] - lam * p[:, :, 1]
        return jnp.einsum('bhqk,bkhe->bqhe', a.astype(v.dtype), v)

    out = lax.map(one_block, (q_blocks, jnp.arange(n_blk)))
    return out.transpose(1, 0, 2, 3, 4).reshape(b, s, h, v.shape[-1])


def memory_cross_attention(q, mk, mv):
    scale = q.shape[-1] ** -0.5
    sc = jnp.einsum('bshd,bmhd->bhsm', q, mk, preferred_element_type=jnp.float32) * scale
    p = jax.nn.softmax(sc, axis=-1)
    return jnp.einsum('bhsm,bmhd->bshd', p.astype(mv.dtype), mv)


def hybrid_layer(x, mem, layer_idx, w_in, b_gate, pool_w, pool_scale, lambda_q1, lambda_k1,
                 lambda_q2, lambda_k2, diff_norm_g, w_mem_kv, w_branch, w_out, ln_out_g, ln_out_b):
    b, s, d = x.shape
    proj = jnp.einsum('bsd,de->bse', x, w_in)
    offs = [0]
    for wdt in IN_WIDTHS:
        offs.append(offs[-1] + wdt)
    parts = [proj[..., offs[i]:offs[i + 1]] for i in range(len(IN_WIDTHS))]
    pool_u, pool_z, dq_, dk_, dv_, diff_z, mq_, mem_z, gates = parts

    u = pool_u.reshape(b, s, POOL_GROUPS, POOL_GROUP_DIM)
    pooled = causal_multiscale_pool(u)
    pooled = jnp.einsum('bsgc,gce->bsge', pooled, pool_w).reshape(b, s, BRANCH_WIDTH) * pool_scale
    o_pool = pooled * jax.nn.silu(pool_z)

    lam_init = lambda_init_fn(layer_idx)
    lam = (jnp.exp(jnp.sum(lambda_q1.astype(jnp.float32) * lambda_k1.astype(jnp.float32)))
           - jnp.exp(jnp.sum(lambda_q2.astype(jnp.float32) * lambda_k2.astype(jnp.float32)))
           + lam_init)
    q = dq_.reshape(b, s, DIFF_HEADS, 2, DIFF_QK_DIM)
    k = dk_.reshape(b, s, DIFF_HEADS, 2, DIFF_QK_DIM)
    v = dv_.reshape(b, s, DIFF_HEADS, DIFF_V_DIM)
    att = differential_attention(q, k, v, lam)
    att = rms_norm(att, diff_norm_g) * (1.0 - lam_init)
    o_diff = att.reshape(b, s, BRANCH_WIDTH) * jax.nn.silu(diff_z)

    mkv = jnp.einsum('bmd,de->bme', mem, w_mem_kv)
    mk = mkv[..., :BRANCH_WIDTH].reshape(b, MEM_LEN, MEM_HEADS, MEM_HEAD_DIM)
    mv = mkv[..., BRANCH_WIDTH:].reshape(b, MEM_LEN, MEM_HEADS, MEM_HEAD_DIM)
    mq = mq_.reshape(b, s, MEM_HEADS, MEM_HEAD_DIM)
    o_mem = memory_cross_attention(mq, mk, mv).reshape(b, s, BRANCH_WIDTH) * jax.nn.silu(mem_z)

    o = jnp.stack([o_pool, o_diff, o_mem], axis=2)
    y = jnp.einsum('bsnc,ncd->bsnd', o, w_branch)
    g = jax.nn.sigmoid((gates + b_gate).reshape(b, s, N_BRANCH, d))
    merged = jnp.sum(g * y, axis=2)
    out = jnp.einsum('bsd,de->bse', merged, w_out)
    return layer_norm(DEEPNORM_ALPHA * x + out, ln_out_g, ln_out_b)


def setup_inputs(seed: int = 0) -> dict:
    key = jax.random.key(seed)
    ks = jax.random.split(key, 20)
    f32 = jnp.float32
    d = D_MODEL
    beta = DEEPNORM_BETA
    seg_scale = (beta, 1.0, 1.0, 1.0, beta, 1.0, 1.0, 1.0, 1.0)
    col_scale = jnp.concatenate([jnp.full((wdt,), sc, f32) for wdt, sc in zip(IN_WIDTHS, seg_scale)])
    w_in = jax.random.normal(ks[2], (DEPTH, d, IN_TOTAL), f32) * (d ** -0.5) * col_scale
    kv_scale = jnp.concatenate([jnp.ones((BRANCH_WIDTH,), f32), jnp.full((BRANCH_WIDTH,), beta, f32)])
    w_mem_kv = jax.random.normal(ks[3], (DEPTH, d, 2 * BRANCH_WIDTH), f32) * (d ** -0.5) * kv_scale
    return {
        "x": jax.random.normal(ks[0], (BATCH, SEQ, d), f32),
        "mem": jax.random.normal(ks[1], (BATCH, MEM_LEN, d), f32),
        "ln_in_g": 1.0 + 0.02 * jax.random.normal(ks[4], (d,), f32),
        "ln_in_b": 0.02 * jax.random.normal(ks[5], (d,), f32),
        "w_in": w_in,
        "b_gate": 0.01 * jax.random.normal(ks[6], (DEPTH, N_BRANCH * d), f32),
        "pool_w": jax.random.normal(ks[7], (DEPTH, POOL_GROUPS, POOL_GROUP_DIM, POOL_GROUP_DIM), f32) * (POOL_GROUP_DIM ** -0.5),
        "pool_scale": 1.0 + 0.02 * jax.random.normal(ks[8], (DEPTH, BRANCH_WIDTH), f32),
        "lambda_q1": 0.1 * jax.random.normal(ks[9], (DEPTH, DIFF_QK_DIM), f32),
        "lambda_k1": 0.1 * jax.random.normal(ks[10], (DEPTH, DIFF_QK_DIM), f32),
        "lambda_q2": 0.1 * jax.random.normal(ks[11], (DEPTH, DIFF_QK_DIM), f32),
        "lambda_k2": 0.1 * jax.random.normal(ks[12], (DEPTH, DIFF_QK_DIM), f32),
        "diff_norm_g": 1.0 + 0.02 * jax.random.normal(ks[13], (DEPTH, DIFF_V_DIM), f32),
        "w_mem_kv": w_mem_kv,
        "w_branch": jax.random.normal(ks[14], (DEPTH, N_BRANCH, BRANCH_WIDTH, d), f32) * (BRANCH_WIDTH ** -0.5) * beta,
        "w_out": jax.random.normal(ks[15], (DEPTH, d, d), f32) * (d ** -0.5) * beta,
        "ln_out_g": 1.0 + 0.02 * jax.random.normal(ks[16], (DEPTH, d), f32),
        "ln_out_b": 0.02 * jax.random.normal(ks[17], (DEPTH, d), f32),
    }


def reference(x, mem, ln_in_g, ln_in_b, w_in, b_gate, pool_w, pool_scale, lambda_q1, lambda_k1,
              lambda_q2, lambda_k2, diff_norm_g, w_mem_kv, w_branch, w_out, ln_out_g, ln_out_b):
    h = layer_norm(x, ln_in_g, ln_in_b)
    for l in range(DEPTH):
        h = hybrid_layer(h, mem, l, w_in[l], b_gate[l], pool_w[l], pool_scale[l],
                         lambda_q1[l], lambda_k1[l], lambda_q2[l], lambda_k2[l], diff_norm_g[l],
                         w_mem_kv[l], w_branch[l], w_out[l], ln_out_g[l], ln_out_b[l])
    return h
```

```python
import functools
import math

import jax
import jax.numpy as jnp
from jax import lax
from jax.experimental import pallas as pl
from jax.experimental.pallas import tpu as pltpu

F32 = jnp.float32
BF16 = jnp.bfloat16

D_MODEL = 1024
BRANCH = 512
MEM_LEN = 256
POOL_WINDOWS = (2, 4, 8, 16)
POOL_GROUP = 128
DIFF_HEADS = 8
DIFF_QK = 32
DIFF_V = 64
MEM_HEADS = 4
MEM_HEAD_DIM = 128
LN_EPS = 1e-5
RMS_EPS = 1e-5
DEPTH = 1
DEEPNORM_ALPHA = (2.0 * DEPTH) ** 0.25
LAM_INIT = 0.8 - 0.6 * math.exp(-0.3 * 0)
DIFF_SCALE = DIFF_QK ** -0.5
MEM_SCALE = MEM_HEAD_DIM ** -0.5
NEG_BIG = -1e30

TOKEN_TILE = 512
Q_TILE = 512
K_TILE = TOKEN_TILE
HALO = 32
LANES = 128
V_AUG = DIFF_V + 16
VMEM_LIMIT = 56 * 1024 * 1024

NT_DIMS = (((1,), (1,)), ((), ()))


def _layer_norm(x, g, b):
    mu = jnp.mean(x, axis=-1, keepdims=True)
    xc = x - mu
    var = jnp.mean(xc * xc, axis=-1, keepdims=True)
    return xc * lax.rsqrt(var + LN_EPS) * g + b


def _silu(x):
    return x * jax.nn.sigmoid(x)


def _const_spec(shape, single_buffer=True):
    n = len(shape)
    mode = pl.Buffered(1) if single_buffer else None
    return pl.BlockSpec(shape, lambda *_: (0,) * n, pipeline_mode=mode)


def _memkv_kernel(mem_ref, w_ref, mk_ref, mv_ref):
    kv = jnp.dot(mem_ref[0].astype(BF16), w_ref[...], preferred_element_type=F32)
    mk_ref[0] = kv[:, :BRANCH].astype(BF16)
    mv_ref[0] = kv[:, BRANCH:].astype(BF16)


def _memkv(mem, w_kv):
    b = mem.shape[0]
    return pl.pallas_call(
        _memkv_kernel,
        out_shape=(jax.ShapeDtypeStruct((b, MEM_LEN, BRANCH), BF16),
                   jax.ShapeDtypeStruct((b, MEM_LEN, BRANCH), BF16)),
        grid=(b,),
        in_specs=[pl.BlockSpec((1, MEM_LEN, D_MODEL), lambda i: (i, 0, 0)),
                  _const_spec((D_MODEL, 2 * BRANCH))],
        out_specs=(pl.BlockSpec((1, MEM_LEN, BRANCH), lambda i: (i, 0, 0)),
                   pl.BlockSpec((1, MEM_LEN, BRANCH), lambda i: (i, 0, 0))),
        name="memkv",
    )(mem, w_kv)


def _proj_kernel(x_ref, lng_ref, lnb_ref, wmain_ref, wvt_ref, wg_ref, bg_ref, poolw_ref, pscale_ref,
                 mk_ref, mv_ref, wb0_ref, wb2_ref,
                 q_ref, k_ref, vt_ref, sz_ref, part_ref, gd_ref,
                 e1_ref, e2_ref, e4_ref, e8_ref):
    tt = TOKEN_TILE
    s_idx = pl.program_id(1)
    h = _layer_norm(x_ref[0], lng_ref[...], lnb_ref[...])
    hb = h.astype(BF16)

    def proj(seg):
        return jnp.dot(hb, wmain_ref[:, seg * BRANCH:(seg + 1) * BRANCH], preferred_element_type=F32)

    q_ref[0] = (proj(2) * DIFF_SCALE).astype(BF16)
    k_ref[0] = proj(3).astype(BF16)
    vt = lax.dot_general(wvt_ref[...], hb, NT_DIMS, preferred_element_type=F32)
    vt_ref[0, 0] = vt.astype(BF16)
    sz_ref[0] = _silu(proj(4)).astype(BF16)

    u = proj(0)

    @pl.when(s_idx == 0)
    def _():
        e1_ref[0:HALO, :] = jnp.zeros((HALO, BRANCH), F32)

    e1_ref[HALO:HALO + tt, :] = u
    n = HALO + tt
    e2_ref[8:n, :] = e1_ref[8:n, :] + e1_ref[7:n - 1, :]
    e4_ref[16:n, :] = e2_ref[16:n, POOL_GROUP:] + e2_ref[14:n - 2, POOL_GROUP:]
    e8_ref[24:n, :] = e4_ref[24:n, POOL_GROUP:] + e4_ref[20:n - 4, POOL_GROUP:]
    s16 = e8_ref[HALO:n, POOL_GROUP:] + e8_ref[HALO - 8:n - 8, POOL_GROUP:]
    wsum = (e2_ref[HALO:n, 0:POOL_GROUP], e4_ref[HALO:n, 0:POOL_GROUP], e8_ref[HALO:n, 0:POOL_GROUP], s16)
    e1_ref[0:HALO, :] = e1_ref[tt:tt + HALO, :]

    t_pos = s_idx * tt + lax.broadcasted_iota(jnp.int32, (tt, POOL_GROUP), 0)
    pooled = []
    for g, w in enumerate(POOL_WINDOWS):
        cnt = jnp.minimum(t_pos + 1, w).astype(F32)
        pg = wsum[g] / cnt - u[:, g * POOL_GROUP:(g + 1) * POOL_GROUP]
        pooled.append(jnp.dot(pg.astype(BF16), poolw_ref[g], preferred_element_type=F32))
    o_pool = jnp.concatenate(pooled, axis=1) * pscale_ref[...] * _silu(proj(1))

    mq = (proj(5) * MEM_SCALE).astype(BF16)
    heads = []
    for hh in range(MEM_HEADS):
        sl = slice(hh * MEM_HEAD_DIM, (hh + 1) * MEM_HEAD_DIM)
        s = lax.dot_general(mq[:, sl], mk_ref[0, :, sl], NT_DIMS, preferred_element_type=F32)
        p = jnp.exp(s - jnp.max(s, axis=-1, keepdims=True))
        l = jnp.sum(p, axis=-1, keepdims=True)
        heads.append(jnp.dot(p.astype(BF16), mv_ref[0, :, sl], preferred_element_type=F32) / l)
    o_mem = jnp.concatenate(heads, axis=1) * _silu(proj(6))

    def gate(nb):
        sl = slice(nb * D_MODEL, (nb + 1) * D_MODEL)
        return jax.nn.sigmoid(jnp.dot(hb, wg_ref[:, sl], preferred_element_type=F32) + bg_ref[:, sl])

    y_pool = jnp.dot(o_pool.astype(BF16), wb0_ref[...], preferred_element_type=F32)
    y_mem = jnp.dot(o_mem.astype(BF16), wb2_ref[...], preferred_element_type=F32)
    part_ref[0] = (gate(0) * y_pool + gate(2) * y_mem).astype(BF16)
    gd_ref[0] = gate(1).astype(BF16)


def _proj(x, ln_g, ln_b, w_main, w_vt, w_g, b_g, pool_w, pool_scale, mk, mv, wb0, wb2):
    b, s, _ = x.shape
    tt = TOKEN_TILE
    ns = s // tt
    tok = lambda width: pl.BlockSpec((1, tt, width), lambda bi, si: (bi, si, 0))
    per_batch = pl.BlockSpec((1, MEM_LEN, BRANCH), lambda bi, si: (bi, 0, 0))
    return pl.pallas_call(
        _proj_kernel,
        out_shape=(jax.ShapeDtypeStruct((b, s, BRANCH), BF16),
                   jax.ShapeDtypeStruct((b, s, BRANCH), BF16),
                   jax.ShapeDtypeStruct((b, ns, BRANCH, tt), BF16),
                   jax.ShapeDtypeStruct((b, s, BRANCH), BF16),
                   jax.ShapeDtypeStruct((b, s, D_MODEL), BF16),
                   jax.ShapeDtypeStruct((b, s, D_MODEL), BF16)),
        grid=(b, ns),
        in_specs=[tok(D_MODEL),
                  _const_spec((1, D_MODEL)), _const_spec((1, D_MODEL)),
                  _const_spec((D_MODEL, 7 * BRANCH)), _const_spec((BRANCH, D_MODEL)),
                  _const_spec((D_MODEL, 3 * D_MODEL)), _const_spec((1, 3 * D_MODEL)),
                  _const_spec((len(POOL_WINDOWS), POOL_GROUP, POOL_GROUP)), _const_spec((1, BRANCH)),
                  per_batch, per_batch,
                  _const_spec((BRANCH, D_MODEL)), _const_spec((BRANCH, D_MODEL))],
        out_specs=(tok(BRANCH), tok(BRANCH),
                   pl.BlockSpec((1, 1, BRANCH, tt), lambda bi, si: (bi, si, 0, 0)),
                   tok(BRANCH), tok(D_MODEL), tok(D_MODEL)),
        scratch_shapes=[pltpu.VMEM((HALO + tt, BRANCH), F32),
                        pltpu.VMEM((HALO + tt, BRANCH), F32),
                        pltpu.VMEM((HALO + tt, BRANCH - POOL_GROUP), F32),
                        pltpu.VMEM((HALO + tt, BRANCH - 2 * POOL_GROUP), F32)],
        compiler_params=pltpu.CompilerParams(dimension_semantics=("arbitrary", "arbitrary"),
                                             vmem_limit_bytes=VMEM_LIMIT),
        name="proj",
    )(x, ln_g, ln_b, w_main, w_vt, w_g, b_g, pool_w, pool_scale, mk, mv, wb0, wb2)


def _attn_kernel(q_ref, k_ref, vt_ref, bias_ref, slope_ref, lam_ref, g_ref, o_ref, qt_s, m_s, acc_s):
    tq, tk = Q_TILE, K_TILE
    i = pl.program_id(2)

    qt = q_ref[0].astype(F32).T.astype(BF16)
    row = lax.broadcasted_iota(jnp.int32, (16, 2 * tq), 0)
    for h in range(2):
        qt_s[h] = jnp.zeros((2 * LANES, 2 * tq), BF16)
        base = 2 * DIFF_QK * h
        qt_s[h, base:base + DIFF_QK, 0:tq] = qt[base:base + DIFF_QK]
        qt_s[h, base + DIFF_QK:base + 2 * DIFF_QK, tq:2 * tq] = qt[base + DIFF_QK:base + 2 * DIFF_QK]
        qt_s[h, LANES:LANES + 16, :] = jnp.where((row == 2 * h) | (row == 2 * h + 1), 1.0, 0.0).astype(BF16)
    m_s[...] = jnp.full(m_s.shape, NEG_BIG, F32)
    acc_s[...] = jnp.zeros(acc_s.shape, F32)

    ones_rows = jnp.where(lax.broadcasted_iota(jnp.int32, (16, tk), 0) == 0, 1.0, 0.0).astype(BF16)
    slope_tk = [jnp.concatenate([slope_ref[0, h:h + 1, :]] * (tq // LANES), axis=1) for h in range(2)]

    def step(j, masked):
        kblk = k_ref[0, pl.ds(pl.multiple_of(j * tk, tk), tk), :]
        lhs = jnp.concatenate([kblk, bias_ref[0]], axis=1)
        vblk = vt_ref[0, j]
        dj = jnp.full((1, tq), j - i, jnp.int32).astype(F32)
        if masked:
            keep = (lax.broadcasted_iota(jnp.int32, (tk, tq), 0) <= lax.broadcasted_iota(jnp.int32, (tk, tq), 1))
        for h in range(2):
            st2 = jnp.dot(lhs, qt_s[h], preferred_element_type=F32)
            off = slope_tk[h] * dj
            vaug = jnp.concatenate([vblk[DIFF_V * h:DIFF_V * (h + 1)], ones_rows], axis=0)
            for mm in range(2):
                idx = 2 * h + mm
                st = st2[:, mm * tq:(mm + 1) * tq]
                if masked:
                    st = jnp.where(keep, st, NEG_BIG)
                m_old = m_s[idx:idx + 1, :]
                m_new = jnp.maximum(m_old, jnp.max(st, axis=0, keepdims=True) + off)
                alpha = jnp.exp(m_old - m_new)
                p = jnp.exp(st - (m_new - off)).astype(BF16)
                pv = jnp.dot(vaug, p, preferred_element_type=F32)
                acc_s[idx] = alpha * acc_s[idx] + pv
                m_s[idx:idx + 1, :] = m_new

    def body(j, carry):
        step(j, False)
        return carry

    lax.fori_loop(0, i, body, 0)
    step(i, True)

    lamv = lam_ref[...]
    lam = (jnp.exp(jnp.sum(lamv[0:1] * lamv[1:2], axis=1, keepdims=True))
           - jnp.exp(jnp.sum(lamv[2:3] * lamv[3:4], axis=1, keepdims=True)) + LAM_INIT)
    gain = jnp.concatenate([g_ref[...]] * (tq // LANES), axis=1)
    ys = []
    for h in range(2):
        a0 = acc_s[2 * h]
        a1 = acc_s[2 * h + 1]
        a = a0[0:DIFF_V] / a0[DIFF_V:DIFF_V + 1] - lam * (a1[0:DIFF_V] / a1[DIFF_V:DIFF_V + 1])
        ms = jnp.mean(a * a, axis=0, keepdims=True)
        ys.append(a * lax.rsqrt(ms + RMS_EPS) * gain * (1.0 - LAM_INIT))
    o_ref[0] = jnp.concatenate(ys, axis=0).T.astype(BF16)


def _attn(q, k, vt, bias, slope_tk, lam_vecs, gain):
    b, s, _ = q.shape
    tq, tk = Q_TILE, K_TILE
    pairs = DIFF_HEADS // 2
    return pl.pallas_call(
        _attn_kernel,
        out_shape=jax.ShapeDtypeStruct((b, s, BRANCH), BF16),
        grid=(b, pairs, s // tq),
        in_specs=[pl.BlockSpec((1, tq, LANES), lambda bi, hp, qi: (bi, qi, hp)),
                  pl.BlockSpec((1, s, LANES), lambda bi, hp, qi: (bi, 0, hp)),
                  pl.BlockSpec((1, s // tk, LANES, tk), lambda bi, hp, qi: (bi, 0, hp, 0)),
                  pl.BlockSpec((1, tk, LANES), lambda bi, hp, qi: (hp, 0, 0)),
                  pl.BlockSpec((1, 8, LANES), lambda bi, hp, qi: (hp, 0, 0)),
                  _const_spec((8, LANES), single_buffer=False),
                  _const_spec((DIFF_V, LANES), single_buffer=False)],
        out_specs=pl.BlockSpec((1, tq, LANES), lambda bi, hp, qi: (bi, qi, hp)),
        scratch_shapes=[pltpu.VMEM((2, 2 * LANES, 2 * tq), BF16),
                        pltpu.VMEM((8, tq), F32),
                        pltpu.VMEM((4, V_AUG, tq), F32)],
        compiler_params=pltpu.CompilerParams(dimension_semantics=("parallel", "parallel", "parallel"),
                                             vmem_limit_bytes=VMEM_LIMIT),
        name="attn",
    )(q, k, vt, bias, slope_tk, lam_vecs, gain)


def _epi_kernel(x_ref, att_ref, sz_ref, part_ref, gd_ref, lng_ref, lnb_ref, wb1_ref, wout_ref, og_ref, ob_ref,
                o_ref):
    h = _layer_norm(x_ref[0], lng_ref[...], lnb_ref[...])
    o_diff = (att_ref[0].astype(F32) * sz_ref[0].astype(F32)).astype(BF16)
    y_diff = jnp.dot(o_diff, wb1_ref[...], preferred_element_type=F32)
    merged = part_ref[0].astype(F32) + gd_ref[0].astype(F32) * y_diff
    out = jnp.dot(merged.astype(BF16), wout_ref[...], preferred_element_type=F32)
    o_ref[0] = _layer_norm(DEEPNORM_ALPHA * h + out, og_ref[...], ob_ref[...])


def _epi(x, att, sz, part, gd, ln_g, ln_b, wb1, w_out, out_g, out_b):
    b, s, _ = x.shape
    tt = TOKEN_TILE
    tok = lambda width: pl.BlockSpec((1, tt, width), lambda bi, si: (bi, si, 0))
    return pl.pallas_call(
        _epi_kernel,
        out_shape=jax.ShapeDtypeStruct((b, s, D_MODEL), F32),
        grid=(b, s // tt),
        in_specs=[tok(D_MODEL), tok(BRANCH), tok(BRANCH), tok(D_MODEL), tok(D_MODEL),
                  _const_spec((1, D_MODEL)), _const_spec((1, D_MODEL)),
                  _const_spec((BRANCH, D_MODEL)), _const_spec((D_MODEL, D_MODEL)),
                  _const_spec((1, D_MODEL)), _const_spec((1, D_MODEL))],
        out_specs=tok(D_MODEL),
        compiler_params=pltpu.CompilerParams(dimension_semantics=("parallel", "parallel"),
                                             vmem_limit_bytes=VMEM_LIMIT),
        name="epi",
    )(x, att, sz, part, gd, ln_g, ln_b, wb1, w_out, out_g, out_b)


def _alibi_tables():
    pos = jnp.arange(K_TILE, dtype=jnp.int32)
    lo = (pos % 256).astype(F32)
    hi = (pos - pos % 256).astype(F32)
    slopes = [2.0 ** (-8.0 * (h + 1) / DIFF_HEADS) for h in range(DIFF_HEADS)]
    bias = jnp.zeros((DIFF_HEADS // 2, K_TILE, LANES), F32)
    slope_tk = jnp.zeros((DIFF_HEADS // 2, 8, LANES), F32)
    for hp in range(DIFF_HEADS // 2):
        for h in range(2):
            sl = slopes[2 * hp + h]
            bias = bias.at[hp, :, 2 * h].set(sl * lo).at[hp, :, 2 * h + 1].set(sl * hi)
            slope_tk = slope_tk.at[hp, h, :].set(sl * K_TILE)
    return bias.astype(BF16), slope_tk


def kernel(x, mem, ln_in_g, ln_in_b, w_in, b_gate, pool_w, pool_scale, lambda_q1, lambda_k1, lambda_q2, lambda_k2,
           diff_norm_g, w_mem_kv, w_branch, w_out, ln_out_g, ln_out_b):
    assert w_in.shape[0] == DEPTH == 1
    w = w_in[0]
    seg = lambda i: w[:, i * BRANCH:(i + 1) * BRANCH]
    w_main = jnp.concatenate([seg(0), seg(1), seg(2), seg(3), seg(5), seg(6), seg(7)], axis=1).astype(BF16)
    w_vt = seg(4).T.astype(BF16)
    w_g = w[:, 8 * BRANCH:].astype(BF16)
    row = lambda v: v.reshape(1, -1).astype(F32)

    mk, mv = _memkv(mem, w_mem_kv[0].astype(BF16))
    q, k, vt, sz, part, gd = _proj(
        x, row(ln_in_g), row(ln_in_b), w_main, w_vt, w_g, row(b_gate[0]), pool_w[0].astype(BF16),
        row(pool_scale[0]), mk, mv, w_branch[0, 0].astype(BF16), w_branch[0, 2].astype(BF16))

    bias, slope_tk = _alibi_tables()
    lam_vecs = jnp.zeros((8, LANES), F32)
    for r, v in enumerate((lambda_q1, lambda_k1, lambda_q2, lambda_k2)):
        lam_vecs = lam_vecs.at[r, :DIFF_QK].set(v[0].astype(F32))
    gain = jnp.broadcast_to(diff_norm_g[0].astype(F32)[:, None], (DIFF_V, LANES))
    att = _attn(q, k, vt, bias, slope_tk, lam_vecs, gain)

    return _epi(x, att, sz, part, gd, row(ln_in_g), row(ln_in_b), w_branch[0, 1].astype(BF16),
                w_out[0].astype(BF16), row(ln_out_g[0]), row(ln_out_b[0]))
```

```python
import math
import struct

import jax
import jax.numpy as jnp
import numpy as np
from jax import lax
from jax.experimental import pallas as pl
from jax.experimental.pallas import tpu as pltpu

F32 = jnp.float32
BF16 = jnp.bfloat16

D_MODEL = 1024
BRANCH = 512
MEM_LEN = 256
POOL_WINDOWS = (2, 4, 8, 16)
POOL_GROUP = 128
DIFF_HEADS = 8
DIFF_QK = 32
DIFF_V = 64
MEM_HEADS = 4
MEM_HEAD_DIM = 128
LN_EPS = 1e-5
RMS_EPS = 1e-5
DEPTH = 1
DEEPNORM_ALPHA = (2.0 * DEPTH) ** 0.25
LAM_INIT = 0.8 - 0.6 * math.exp(-0.3 * 0)
LOG2E = math.log2(math.e)
DIFF_SCALE = DIFF_QK ** -0.5 * LOG2E


def _bf16_terms(x, n):
    terms = []
    for _ in range(n):
        bits = struct.unpack("<I", struct.pack("<f", x))[0]
        bits = (bits + 0x7FFF + ((bits >> 16) & 1)) & 0xFFFF0000
        t = struct.unpack("<f", struct.pack("<I", bits))[0]
        terms.append(t)
        x -= t
    return terms


LOG2E_TERMS = _bf16_terms(LOG2E, 3)
MEM_SCALE = MEM_HEAD_DIM ** -0.5
NEG_BIG = -1e30

TOKEN_TILE = 512
Q_TILE = 512
K_TILE = TOKEN_TILE
HALO = 32
LANES = 128
V_AUG = DIFF_V + 16
ALIBI_COLS = 2 * len(LOG2E_TERMS)
VMEM_LIMIT = 56 * 1024 * 1024

NT_DIMS = (((1,), (1,)), ((), ()))


def _layer_norm(x, g, b):
    mu = jnp.mean(x, axis=-1, keepdims=True)
    xc = x - mu
    var = jnp.mean(xc * xc, axis=-1, keepdims=True)
    return xc * lax.rsqrt(var + LN_EPS) * g + b


def _silu(x):
    return x * jax.nn.sigmoid(x)


def _const_spec(shape, single_buffer=True):
    n = len(shape)
    mode = pl.Buffered(1) if single_buffer else None
    return pl.BlockSpec(shape, lambda *_: (0,) * n, pipeline_mode=mode)


def _memkv_kernel(mem_ref, w_ref, mk_ref, mv_ref):
    kv = jnp.dot(mem_ref[0].astype(BF16), w_ref[...], preferred_element_type=F32)
    mk_ref[0] = kv[:, :BRANCH].astype(BF16)
    mv_ref[0] = kv[:, BRANCH:].astype(BF16)


def _memkv(mem, w_kv):
    b = mem.shape[0]
    return pl.pallas_call(
        _memkv_kernel,
        out_shape=(jax.ShapeDtypeStruct((b, MEM_LEN, BRANCH), BF16),
                   jax.ShapeDtypeStruct((b, MEM_LEN, BRANCH), BF16)),
        grid=(b,),
        in_specs=[pl.BlockSpec((1, MEM_LEN, D_MODEL), lambda i: (i, 0, 0)),
                  _const_spec((D_MODEL, 2 * BRANCH))],
        out_specs=(pl.BlockSpec((1, MEM_LEN, BRANCH), lambda i: (i, 0, 0)),
                   pl.BlockSpec((1, MEM_LEN, BRANCH), lambda i: (i, 0, 0))),
        name="memkv",
    )(mem, w_kv)


def _proj_kernel(x_ref, lng_ref, lnb_ref, wmain_ref, wvt_ref, wg_ref, bg_ref, poolw_ref, pscale_ref,
                 mk_ref, mv_ref, wb0_ref, wb2_ref,
                 q_ref, k_ref, vt_ref, sz_ref, part_ref, gd_ref,
                 e1_ref, e2_ref, e4_ref, e8_ref):
    tt = TOKEN_TILE
    s_idx = pl.program_id(1)
    h = _layer_norm(x_ref[0], lng_ref[...], lnb_ref[...])
    hb = h.astype(BF16)

    def proj(seg):
        return jnp.dot(hb, wmain_ref[:, seg * BRANCH:(seg + 1) * BRANCH], preferred_element_type=F32)

    q_ref[0] = (proj(2) * DIFF_SCALE).astype(BF16)
    k_ref[0] = proj(3).astype(BF16)
    vt = lax.dot_general(wvt_ref[...], hb, NT_DIMS, preferred_element_type=F32)
    vt_ref[0, 0] = vt.astype(BF16)
    sz_ref[0] = _silu(proj(4)).astype(BF16)

    u = proj(0)

    @pl.when(s_idx == 0)
    def _():
        e1_ref[0:HALO, :] = jnp.zeros((HALO, BRANCH), F32)

    e1_ref[HALO:HALO + tt, :] = u
    n = HALO + tt
    e2_ref[8:n, :] = e1_ref[8:n, :] + e1_ref[7:n - 1, :]
    e4_ref[16:n, :] = e2_ref[16:n, POOL_GROUP:] + e2_ref[14:n - 2, POOL_GROUP:]
    e8_ref[24:n, :] = e4_ref[24:n, POOL_GROUP:] + e4_ref[20:n - 4, POOL_GROUP:]
    s16 = e8_ref[HALO:n, POOL_GROUP:] + e8_ref[HALO - 8:n - 8, POOL_GROUP:]
    wsum = (e2_ref[HALO:n, 0:POOL_GROUP], e4_ref[HALO:n, 0:POOL_GROUP], e8_ref[HALO:n, 0:POOL_GROUP], s16)
    e1_ref[0:HALO, :] = e1_ref[tt:tt + HALO, :]

    t_pos = s_idx * tt + lax.broadcasted_iota(jnp.int32, (tt, POOL_GROUP), 0)
    pooled = []
    for g, w in enumerate(POOL_WINDOWS):
        cnt = jnp.minimum(t_pos + 1, w).astype(F32)
        pg = wsum[g] / cnt - u[:, g * POOL_GROUP:(g + 1) * POOL_GROUP]
        pooled.append(jnp.dot(pg.astype(BF16), poolw_ref[g], preferred_element_type=F32))
    o_pool = jnp.concatenate(pooled, axis=1) * pscale_ref[...] * _silu(proj(1))

    mq = (proj(5) * MEM_SCALE).astype(BF16)
    heads = []
    for hh in range(MEM_HEADS):
        sl = slice(hh * MEM_HEAD_DIM, (hh + 1) * MEM_HEAD_DIM)
        s = lax.dot_general(mq[:, sl], mk_ref[0, :, sl], NT_DIMS, preferred_element_type=F32)
        p = jnp.exp(s - jnp.max(s, axis=-1, keepdims=True))
        l = jnp.sum(p, axis=-1, keepdims=True)
        heads.append(jnp.dot(p.astype(BF16), mv_ref[0, :, sl], preferred_element_type=F32) / l)
    o_mem = jnp.concatenate(heads, axis=1) * _silu(proj(6))

    def gate(nb):
        sl = slice(nb * D_MODEL, (nb + 1) * D_MODEL)
        return jax.nn.sigmoid(jnp.dot(hb, wg_ref[:, sl], preferred_element_type=F32) + bg_ref[:, sl])

    y_pool = jnp.dot(o_pool.astype(BF16), wb0_ref[...], preferred_element_type=F32)
    y_mem = jnp.dot(o_mem.astype(BF16), wb2_ref[...], preferred_element_type=F32)
    part_ref[0] = (gate(0) * y_pool + gate(2) * y_mem).astype(BF16)
    gd_ref[0] = gate(1).astype(BF16)


def _proj(x, ln_g, ln_b, w_main, w_vt, w_g, b_g, pool_w, pool_scale, mk, mv, wb0, wb2):
    b, s, _ = x.shape
    tt = TOKEN_TILE
    ns = s // tt
    tok = lambda width: pl.BlockSpec((1, tt, width), lambda bi, si: (bi, si, 0))
    per_batch = pl.BlockSpec((1, MEM_LEN, BRANCH), lambda bi, si: (bi, 0, 0))
    return pl.pallas_call(
        _proj_kernel,
        out_shape=(jax.ShapeDtypeStruct((b, s, BRANCH), BF16),
                   jax.ShapeDtypeStruct((b, s, BRANCH), BF16),
                   jax.ShapeDtypeStruct((b, ns, BRANCH, tt), BF16),
                   jax.ShapeDtypeStruct((b, s, BRANCH), BF16),
                   jax.ShapeDtypeStruct((b, s, D_MODEL), BF16),
                   jax.ShapeDtypeStruct((b, s, D_MODEL), BF16)),
        grid=(b, ns),
        in_specs=[tok(D_MODEL),
                  _const_spec((1, D_MODEL)), _const_spec((1, D_MODEL)),
                  _const_spec((D_MODEL, 7 * BRANCH)), _const_spec((BRANCH, D_MODEL)),
                  _const_spec((D_MODEL, 3 * D_MODEL)), _const_spec((1, 3 * D_MODEL)),
                  _const_spec((len(POOL_WINDOWS), POOL_GROUP, POOL_GROUP)), _const_spec((1, BRANCH)),
                  per_batch, per_batch,
                  _const_spec((BRANCH, D_MODEL)), _const_spec((BRANCH, D_MODEL))],
        out_specs=(tok(BRANCH), tok(BRANCH),
                   pl.BlockSpec((1, 1, BRANCH, tt), lambda bi, si: (bi, si, 0, 0)),
                   tok(BRANCH), tok(D_MODEL), tok(D_MODEL)),
        scratch_shapes=[pltpu.VMEM((HALO + tt, BRANCH), F32),
                        pltpu.VMEM((HALO + tt, BRANCH), F32),
                        pltpu.VMEM((HALO + tt, BRANCH - POOL_GROUP), F32),
                        pltpu.VMEM((HALO + tt, BRANCH - 2 * POOL_GROUP), F32)],
        compiler_params=pltpu.CompilerParams(dimension_semantics=("arbitrary", "arbitrary"),
                                             vmem_limit_bytes=VMEM_LIMIT),
        name="proj",
    )(x, ln_g, ln_b, w_main, w_vt, w_g, b_g, pool_w, pool_scale, mk, mv, wb0, wb2)


def _attn_kernel(q_ref, k_ref, vt_ref, bias_ref, slope_ref, lam_ref, g_ref, o_ref, qt_s, m_s, acc_s,
                 sa_s, sb_s, tma_s, tmb_s):
    tq, tk = Q_TILE, K_TILE
    i = pl.program_id(2)

    qt = q_ref[0].astype(F32).T.astype(BF16)
    row = lax.broadcasted_iota(jnp.int32, (16, 2 * tq), 0)
    for h in range(2):
        qt_s[h] = jnp.zeros((2 * LANES, 2 * tq), BF16)
        base = 2 * DIFF_QK * h
        qt_s[h, base:base + DIFF_QK, 0:tq] = qt[base:base + DIFF_QK]
        qt_s[h, base + DIFF_QK:base + 2 * DIFF_QK, tq:2 * tq] = qt[base + DIFF_QK:base + 2 * DIFF_QK]
        alibi_rows = jnp.zeros((16, 2 * tq), F32)
        for t, c in enumerate(LOG2E_TERMS):
            r0 = ALIBI_COLS * h + 2 * t
            alibi_rows = jnp.where((row == r0) | (row == r0 + 1), c, alibi_rows)
        qt_s[h, LANES:LANES + 16, :] = alibi_rows.astype(BF16)
    m_s[...] = jnp.full(m_s.shape, NEG_BIG, F32)
    acc_s[...] = jnp.zeros(acc_s.shape, F32)

    ones_rows = jnp.where(lax.broadcasted_iota(jnp.int32, (16, tk), 0) == 0, 1.0, 0.0).astype(BF16)
    slope_tk = [jnp.concatenate([slope_ref[0, h:h + 1, :]] * (tq // LANES), axis=1) for h in range(2)]

    def score_stage(blk, s_buf, tm_buf, masked):
        kblk = k_ref[0, pl.ds(pl.multiple_of(blk * tk, tk), tk), :]
        lhs = jnp.concatenate([kblk, bias_ref[0]], axis=1)
        if masked:
            kk = lax.broadcasted_iota(jnp.int32, (tk, 2 * tq), 0)
            qq = lax.broadcasted_iota(jnp.int32, (tk, 2 * tq), 1)
            keep = kk <= jnp.where(qq >= tq, qq - tq, qq)
        for h in range(2):
            st2 = jnp.dot(lhs, qt_s[h], preferred_element_type=F32)
            if masked:
                st2 = jnp.where(keep, st2, NEG_BIG)
            s_buf[h] = st2
            tm_buf[h:h + 1, :] = jnp.max(st2, axis=0, keepdims=True)

    def value_stage(blk, s_buf, tm_buf):
        vblk = vt_ref[0, blk]
        dj = jnp.full((1, tq), blk - i, jnp.int32).astype(F32)
        for h in range(2):
            off = slope_tk[h] * dj
            vaug = jnp.concatenate([vblk[DIFF_V * h:DIFF_V * (h + 1)], ones_rows], axis=0)
            for mm in range(2):
                idx = 2 * h + mm
                cols = slice(mm * tq, (mm + 1) * tq)
                m_old = m_s[idx:idx + 1, :]
                m_new = jnp.maximum(m_old, tm_buf[h:h + 1, cols] + off)
                alpha = jnp.exp2(m_old - m_new)
                p = jnp.exp2(s_buf[h, :, cols] - (m_new - off)).astype(BF16)
                pv = jnp.dot(vaug, p, preferred_element_type=F32)
                acc_s[idx] = alpha * acc_s[idx] + pv
                m_s[idx:idx + 1, :] = m_new

    score_stage(i, sa_s, tma_s, True)

    def pair(p, prev):
        score_stage(2 * p, sb_s, tmb_s, False)
        value_stage(prev, sa_s, tma_s)
        score_stage(2 * p + 1, sa_s, tma_s, False)
        value_stage(2 * p, sb_s, tmb_s)
        return 2 * p + 1

    prev = lax.fori_loop(0, i // 2, pair, i)

    @pl.when(i % 2 == 1)
    def _():
        score_stage(i - 1, sb_s, tmb_s, False)
        value_stage(prev, sa_s, tma_s)
        value_stage(i - 1, sb_s, tmb_s)

    @pl.when(i % 2 == 0)
    def _():
        value_stage(prev, sa_s, tma_s)

    lamv = lam_ref[...]
    lam = (jnp.exp(jnp.sum(lamv[0:1] * lamv[1:2], axis=1, keepdims=True))
           - jnp.exp(jnp.sum(lamv[2:3] * lamv[3:4], axis=1, keepdims=True)) + LAM_INIT)
    gain = jnp.concatenate([g_ref[...]] * (tq // LANES), axis=1)
    ys = []
    for h in range(2):
        a0 = acc_s[2 * h]
        a1 = acc_s[2 * h + 1]
        a = a0[0:DIFF_V] / a0[DIFF_V:DIFF_V + 1] - lam * (a1[0:DIFF_V] / a1[DIFF_V:DIFF_V + 1])
        ms = jnp.mean(a * a, axis=0, keepdims=True)
        ys.append(a * lax.rsqrt(ms + RMS_EPS) * gain * (1.0 - LAM_INIT))
    o_ref[0] = jnp.concatenate(ys, axis=0).T.astype(BF16)


def _attn(q, k, vt, bias, slope_tk, lam_vecs, gain):
    b, s, _ = q.shape
    tq, tk = Q_TILE, K_TILE
    pairs = DIFF_HEADS // 2
    return pl.pallas_call(
        _attn_kernel,
        out_shape=jax.ShapeDtypeStruct((b, s, BRANCH), BF16),
        grid=(b, pairs, s // tq),
        in_specs=[pl.BlockSpec((1, tq, LANES), lambda bi, hp, qi: (bi, qi, hp)),
                  pl.BlockSpec((1, s, LANES), lambda bi, hp, qi: (bi, 0, hp)),
                  pl.BlockSpec((1, s // tk, LANES, tk), lambda bi, hp, qi: (bi, 0, hp, 0)),
                  pl.BlockSpec((1, tk, LANES), lambda bi, hp, qi: (hp, 0, 0)),
                  pl.BlockSpec((1, 8, LANES), lambda bi, hp, qi: (hp, 0, 0)),
                  _const_spec((8, LANES), single_buffer=False),
                  _const_spec((DIFF_V, LANES), single_buffer=False)],
        out_specs=pl.BlockSpec((1, tq, LANES), lambda bi, hp, qi: (bi, qi, hp)),
        scratch_shapes=[pltpu.VMEM((2, 2 * LANES, 2 * tq), BF16),
                        pltpu.VMEM((8, tq), F32),
                        pltpu.VMEM((4, V_AUG, tq), F32),
                        pltpu.VMEM((2, tk, 2 * tq), F32),
                        pltpu.VMEM((2, tk, 2 * tq), F32),
                        pltpu.VMEM((8, 2 * tq), F32),
                        pltpu.VMEM((8, 2 * tq), F32)],
        compiler_params=pltpu.CompilerParams(dimension_semantics=("parallel", "parallel", "parallel"),
                                             vmem_limit_bytes=VMEM_LIMIT),
        name="attn",
    )(q, k, vt, bias, slope_tk, lam_vecs, gain)


def _epi_kernel(x_ref, att_ref, sz_ref, part_ref, gd_ref, lng_ref, lnb_ref, wb1_ref, wout_ref, og_ref, ob_ref,
                o_ref):
    h = _layer_norm(x_ref[0], lng_ref[...], lnb_ref[...])
    o_diff = (att_ref[0].astype(F32) * sz_ref[0].astype(F32)).astype(BF16)
    y_diff = jnp.dot(o_diff, wb1_ref[...], preferred_element_type=F32)
    merged = part_ref[0].astype(F32) + gd_ref[0].astype(F32) * y_diff
    out = jnp.dot(merged.astype(BF16), wout_ref[...], preferred_element_type=F32)
    o_ref[0] = _layer_norm(DEEPNORM_ALPHA * h + out, og_ref[...], ob_ref[...])


def _epi(x, att, sz, part, gd, ln_g, ln_b, wb1, w_out, out_g, out_b):
    b, s, _ = x.shape
    tt = TOKEN_TILE
    tok = lambda width: pl.BlockSpec((1, tt, width), lambda bi, si: (bi, si, 0))
    return pl.pallas_call(
        _epi_kernel,
        out_shape=jax.ShapeDtypeStruct((b, s, D_MODEL), F32),
        grid=(b, s // tt),
        in_specs=[tok(D_MODEL), tok(BRANCH), tok(BRANCH), tok(D_MODEL), tok(D_MODEL),
                  _const_spec((1, D_MODEL)), _const_spec((1, D_MODEL)),
                  _const_spec((BRANCH, D_MODEL)), _const_spec((D_MODEL, D_MODEL)),
                  _const_spec((1, D_MODEL)), _const_spec((1, D_MODEL))],
        out_specs=tok(D_MODEL),
        compiler_params=pltpu.CompilerParams(dimension_semantics=("parallel", "parallel"),
                                             vmem_limit_bytes=VMEM_LIMIT),
        name="epi",
    )(x, att, sz, part, gd, ln_g, ln_b, wb1, w_out, out_g, out_b)


def _alibi_tables():
    pos = np.arange(K_TILE)
    lo = (pos % 256).astype(np.float32)
    hi = (pos - pos % 256).astype(np.float32)
    slopes = [2.0 ** (-8.0 * (h + 1) / DIFF_HEADS) for h in range(DIFF_HEADS)]
    bias = np.zeros((DIFF_HEADS // 2, K_TILE, LANES), np.float32)
    slope_tk = np.zeros((DIFF_HEADS // 2, 8, LANES), np.float32)
    for hp in range(DIFF_HEADS // 2):
        for h in range(2):
            sl = slopes[2 * hp + h]
            for t in range(len(LOG2E_TERMS)):
                bias[hp, :, ALIBI_COLS * h + 2 * t] = sl * lo
                bias[hp, :, ALIBI_COLS * h + 2 * t + 1] = sl * hi
            slope_tk[hp, h, :] = sl * K_TILE * LOG2E
    return jnp.asarray(bias, BF16), jnp.asarray(slope_tk, F32)


def kernel(x, mem, ln_in_g, ln_in_b, w_in, b_gate, pool_w, pool_scale, lambda_q1, lambda_k1, lambda_q2, lambda_k2,
           diff_norm_g, w_mem_kv, w_branch, w_out, ln_out_g, ln_out_b):
    assert w_in.shape[0] == DEPTH == 1
    w = w_in[0]
    seg = lambda i: w[:, i * BRANCH:(i + 1) * BRANCH]
    w_main = jnp.concatenate([seg(0), seg(1), seg(2), seg(3), seg(5), seg(6), seg(7)], axis=1).astype(BF16)
    w_vt = seg(4).T.astype(BF16)
    w_g = w[:, 8 * BRANCH:].astype(BF16)
    row = lambda v: v.reshape(1, -1).astype(F32)

    mk, mv = _memkv(mem, w_mem_kv[0].astype(BF16))
    q, k, vt, sz, part, gd = _proj(
        x, row(ln_in_g), row(ln_in_b), w_main, w_vt, w_g, row(b_gate[0]), pool_w[0].astype(BF16),
        row(pool_scale[0]), mk, mv, w_branch[0, 0].astype(BF16), w_branch[0, 2].astype(BF16))

    bias, slope_tk = _alibi_tables()
    lam_vecs = jnp.concatenate([lambda_q1, lambda_k1, lambda_q2, lambda_k2], axis=0).astype(F32)
    lam_vecs = jnp.pad(lam_vecs, ((0, 4), (0, LANES - DIFF_QK)))
    gain = jnp.broadcast_to(diff_norm_g[0].astype(F32)[:, None], (DIFF_V, LANES))
    att = _attn(q, k, vt, bias, slope_tk, lam_vecs, gain)

    return _epi(x, att, sz, part, gd, row(ln_in_g), row(ln_in_b), w_branch[0, 1].astype(BF16),
                w_out[0].astype(BF16), row(ln_out_g[0]), row(ln_out_b[0]))
```

```python
import math
import struct

import jax
import jax.numpy as jnp
import numpy as np
from jax import lax
from jax.experimental import pallas as pl
from jax.experimental.pallas import tpu as pltpu

F32 = jnp.float32
BF16 = jnp.bfloat16

D_MODEL = 1024
BRANCH = 512
MEM_LEN = 256
POOL_WINDOWS = (2, 4, 8, 16)
POOL_GROUP = 128
DIFF_HEADS = 8
DIFF_QK = 32
DIFF_V = 64
MEM_HEADS = 4
MEM_HEAD_DIM = 128
LN_EPS = 1e-5
RMS_EPS = 1e-5
DEPTH = 1
DEEPNORM_ALPHA = (2.0 * DEPTH) ** 0.25
LAM_INIT = 0.8 - 0.6 * math.exp(-0.3 * 0)
LOG2E = math.log2(math.e)
DIFF_SCALE = DIFF_QK ** -0.5 * LOG2E


def _bf16_terms(x, n):
    terms = []
    for _ in range(n):
        bits = struct.unpack("<I", struct.pack("<f", x))[0]
        bits = (bits + 0x7FFF + ((bits >> 16) & 1)) & 0xFFFF0000
        t = struct.unpack("<f", struct.pack("<I", bits))[0]
        terms.append(t)
        x -= t
    return terms


LOG2E_TERMS = _bf16_terms(LOG2E, 3)
MEM_SCALE = MEM_HEAD_DIM ** -0.5
NEG_BIG = -1e30

TOKEN_TILE = 512
Q_TILE = 512
K_TILE = TOKEN_TILE
CHUNK = 256
HALO = 32
LANES = 128
V_AUG = DIFF_V + 16
ALIBI_COLS = 2 * len(LOG2E_TERMS)
VMEM_LIMIT = 56 * 1024 * 1024

NT_DIMS = (((1,), (1,)), ((), ()))


def _layer_norm(x, g, b):
    mu = jnp.mean(x, axis=-1, keepdims=True)
    xc = x - mu
    var = jnp.mean(xc * xc, axis=-1, keepdims=True)
    return xc * lax.rsqrt(var + LN_EPS) * g + b


def _silu(x):
    return x * jax.nn.sigmoid(x)


def _const_spec(shape, single_buffer=True):
    n = len(shape)
    mode = pl.Buffered(1) if single_buffer else None
    return pl.BlockSpec(shape, lambda *_: (0,) * n, pipeline_mode=mode)


def _memkv_kernel(mem_ref, w_ref, mk_ref, mv_ref):
    kv = jnp.dot(mem_ref[0].astype(BF16), w_ref[...], preferred_element_type=F32)
    mk_ref[0] = kv[:, :BRANCH].astype(BF16)
    mv_ref[0] = kv[:, BRANCH:].astype(BF16)


def _memkv(mem, w_kv):
    b = mem.shape[0]
    return pl.pallas_call(
        _memkv_kernel,
        out_shape=(jax.ShapeDtypeStruct((b, MEM_LEN, BRANCH), BF16),
                   jax.ShapeDtypeStruct((b, MEM_LEN, BRANCH), BF16)),
        grid=(b,),
        in_specs=[pl.BlockSpec((1, MEM_LEN, D_MODEL), lambda i: (i, 0, 0)),
                  _const_spec((D_MODEL, 2 * BRANCH))],
        out_specs=(pl.BlockSpec((1, MEM_LEN, BRANCH), lambda i: (i, 0, 0)),
                   pl.BlockSpec((1, MEM_LEN, BRANCH), lambda i: (i, 0, 0))),
        name="memkv",
    )(mem, w_kv)


def _proj_kernel(x_ref, lng_ref, lnb_ref, wmain_ref, wvt_ref, wg_ref, bg_ref, poolw_ref, pscale_ref,
                 mk_ref, mv_ref, wb0_ref, wb2_ref,
                 q_ref, k_ref, vt_ref, sz_ref, part_ref, gd_ref,
                 e1_ref, e2_ref, e4_ref, e8_ref):
    tt = TOKEN_TILE
    s_idx = pl.program_id(1)
    h = _layer_norm(x_ref[0], lng_ref[...], lnb_ref[...])
    hb = h.astype(BF16)

    def proj(seg):
        return jnp.dot(hb, wmain_ref[:, seg * BRANCH:(seg + 1) * BRANCH], preferred_element_type=F32)

    q_ref[0] = (proj(2) * DIFF_SCALE).astype(BF16)
    k_ref[0] = proj(3).astype(BF16)
    vt = lax.dot_general(wvt_ref[...], hb, NT_DIMS, preferred_element_type=F32)
    vt_ref[0, 0] = vt.astype(BF16)
    sz_ref[0] = _silu(proj(4)).astype(BF16)

    u = proj(0)

    @pl.when(s_idx == 0)
    def _():
        e1_ref[0:HALO, :] = jnp.zeros((HALO, BRANCH), F32)

    e1_ref[HALO:HALO + tt, :] = u
    n = HALO + tt
    e2_ref[8:n, :] = e1_ref[8:n, :] + e1_ref[7:n - 1, :]
    e4_ref[16:n, :] = e2_ref[16:n, POOL_GROUP:] + e2_ref[14:n - 2, POOL_GROUP:]
    e8_ref[24:n, :] = e4_ref[24:n, POOL_GROUP:] + e4_ref[20:n - 4, POOL_GROUP:]
    s16 = e8_ref[HALO:n, POOL_GROUP:] + e8_ref[HALO - 8:n - 8, POOL_GROUP:]
    wsum = (e2_ref[HALO:n, 0:POOL_GROUP], e4_ref[HALO:n, 0:POOL_GROUP], e8_ref[HALO:n, 0:POOL_GROUP], s16)
    e1_ref[0:HALO, :] = e1_ref[tt:tt + HALO, :]

    t_pos = s_idx * tt + lax.broadcasted_iota(jnp.int32, (tt, POOL_GROUP), 0)
    pooled = []
    for g, w in enumerate(POOL_WINDOWS):
        cnt = jnp.minimum(t_pos + 1, w).astype(F32)
        pg = wsum[g] / cnt - u[:, g * POOL_GROUP:(g + 1) * POOL_GROUP]
        pooled.append(jnp.dot(pg.astype(BF16), poolw_ref[g], preferred_element_type=F32))
    o_pool = jnp.concatenate(pooled, axis=1) * pscale_ref[...] * _silu(proj(1))

    mq = (proj(5) * MEM_SCALE).astype(BF16)
    heads = []
    for hh in range(MEM_HEADS):
        sl = slice(hh * MEM_HEAD_DIM, (hh + 1) * MEM_HEAD_DIM)
        s = lax.dot_general(mq[:, sl], mk_ref[0, :, sl], NT_DIMS, preferred_element_type=F32)
        p = jnp.exp(s - jnp.max(s, axis=-1, keepdims=True))
        l = jnp.sum(p, axis=-1, keepdims=True)
        heads.append(jnp.dot(p.astype(BF16), mv_ref[0, :, sl], preferred_element_type=F32) / l)
    o_mem = jnp.concatenate(heads, axis=1) * _silu(proj(6))

    def gate(nb):
        sl = slice(nb * D_MODEL, (nb + 1) * D_MODEL)
        return jax.nn.sigmoid(jnp.dot(hb, wg_ref[:, sl], preferred_element_type=F32) + bg_ref[:, sl])

    y_pool = jnp.dot(o_pool.astype(BF16), wb0_ref[...], preferred_element_type=F32)
    y_mem = jnp.dot(o_mem.astype(BF16), wb2_ref[...], preferred_element_type=F32)
    part_ref[0] = (gate(0) * y_pool + gate(2) * y_mem).astype(BF16)
    gd_ref[0] = gate(1).astype(BF16)


def _proj(x, ln_g, ln_b, w_main, w_vt, w_g, b_g, pool_w, pool_scale, mk, mv, wb0, wb2):
    b, s, _ = x.shape
    tt = TOKEN_TILE
    ns = s // tt
    tok = lambda width: pl.BlockSpec((1, tt, width), lambda bi, si: (bi, si, 0))
    per_batch = pl.BlockSpec((1, MEM_LEN, BRANCH), lambda bi, si: (bi, 0, 0))
    return pl.pallas_call(
        _proj_kernel,
        out_shape=(jax.ShapeDtypeStruct((b, s, BRANCH), BF16),
                   jax.ShapeDtypeStruct((b, s, BRANCH), BF16),
                   jax.ShapeDtypeStruct((b, ns, BRANCH, tt), BF16),
                   jax.ShapeDtypeStruct((b, s, BRANCH), BF16),
                   jax.ShapeDtypeStruct((b, s, D_MODEL), BF16),
                   jax.ShapeDtypeStruct((b, s, D_MODEL), BF16)),
        grid=(b, ns),
        in_specs=[tok(D_MODEL),
                  _const_spec((1, D_MODEL)), _const_spec((1, D_MODEL)),
                  _const_spec((D_MODEL, 7 * BRANCH)), _const_spec((BRANCH, D_MODEL)),
                  _const_spec((D_MODEL, 3 * D_MODEL)), _const_spec((1, 3 * D_MODEL)),
                  _const_spec((len(POOL_WINDOWS), POOL_GROUP, POOL_GROUP)), _const_spec((1, BRANCH)),
                  per_batch, per_batch,
                  _const_spec((BRANCH, D_MODEL)), _const_spec((BRANCH, D_MODEL))],
        out_specs=(tok(BRANCH), tok(BRANCH),
                   pl.BlockSpec((1, 1, BRANCH, tt), lambda bi, si: (bi, si, 0, 0)),
                   tok(BRANCH), tok(D_MODEL), tok(D_MODEL)),
        scratch_shapes=[pltpu.VMEM((HALO + tt, BRANCH), F32),
                        pltpu.VMEM((HALO + tt, BRANCH), F32),
                        pltpu.VMEM((HALO + tt, BRANCH - POOL_GROUP), F32),
                        pltpu.VMEM((HALO + tt, BRANCH - 2 * POOL_GROUP), F32)],
        compiler_params=pltpu.CompilerParams(dimension_semantics=("arbitrary", "arbitrary"),
                                             vmem_limit_bytes=VMEM_LIMIT),
        name="proj",
    )(x, ln_g, ln_b, w_main, w_vt, w_g, b_g, pool_w, pool_scale, mk, mv, wb0, wb2)


def _attn_kernel(q_ref, k_ref, vt_ref, bias_ref, slope_ref, lam_ref, g_ref, o_ref, qt_s, m_s, acc_s,
                 sa_s, sb_s, tma_s, tmb_s):
    tq, tk = Q_TILE, K_TILE
    i = pl.program_id(2)

    qt = q_ref[0].astype(F32).T.astype(BF16)
    row = lax.broadcasted_iota(jnp.int32, (16, 2 * tq), 0)
    for h in range(2):
        qt_s[h] = jnp.zeros((2 * LANES, 2 * tq), BF16)
        base = 2 * DIFF_QK * h
        qt_s[h, base:base + DIFF_QK, 0:tq] = qt[base:base + DIFF_QK]
        qt_s[h, base + DIFF_QK:base + 2 * DIFF_QK, tq:2 * tq] = qt[base + DIFF_QK:base + 2 * DIFF_QK]
        alibi_rows = jnp.zeros((16, 2 * tq), F32)
        for t, c in enumerate(LOG2E_TERMS):
            r0 = ALIBI_COLS * h + 2 * t
            alibi_rows = jnp.where((row == r0) | (row == r0 + 1), c, alibi_rows)
        qt_s[h, LANES:LANES + 16, :] = alibi_rows.astype(BF16)
    m_s[...] = jnp.full(m_s.shape, NEG_BIG, F32)
    acc_s[...] = jnp.zeros(acc_s.shape, F32)

    ones_rows = jnp.where(lax.broadcasted_iota(jnp.int32, (16, tk), 0) == 0, 1.0, 0.0).astype(BF16)
    slope_tk = [jnp.concatenate([slope_ref[0, h:h + 1, :]] * (tq // LANES), axis=1) for h in range(2)]

    cw = CHUNK
    chunks = [(h, n) for h in range(2) for n in range(2 * tq // cw)]

    def score_chunk(lhs, s_buf, tm_buf, h, n, masked):
        cols = slice(n * cw, (n + 1) * cw)
        st = jnp.dot(lhs, qt_s[h, :, cols], preferred_element_type=F32)
        if masked:
            kk = lax.broadcasted_iota(jnp.int32, (tk, cw), 0)
            qq = lax.broadcasted_iota(jnp.int32, (tk, cw), 1) + (n * cw) % tq
            st = jnp.where(kk <= qq, st, NEG_BIG)
        s_buf[h, :, cols] = st
        tm_buf[h:h + 1, cols] = jnp.max(st, axis=0, keepdims=True)

    def value_chunk(vblk, dj, s_buf, tm_buf, h, n):
        idx = 2 * h + (n * cw) // tq
        cols = slice(n * cw, (n + 1) * cw)
        qcols = slice((n * cw) % tq, (n * cw) % tq + cw)
        off = slope_tk[h][:, qcols] * dj
        vaug = jnp.concatenate([vblk[DIFF_V * h:DIFF_V * (h + 1)], ones_rows], axis=0)
        m_old = m_s[idx:idx + 1, qcols]
        m_new = jnp.maximum(m_old, tm_buf[h:h + 1, cols] + off)
        alpha = jnp.exp2(m_old - m_new)
        p = jnp.exp2(s_buf[h, :, cols] - (m_new - off)).astype(BF16)
        pv = jnp.dot(vaug, p, preferred_element_type=F32)
        acc_s[idx, :, qcols] = alpha * acc_s[idx, :, qcols] + pv
        m_s[idx:idx + 1, qcols] = m_new

    def key_lhs(blk):
        kblk = k_ref[0, pl.ds(pl.multiple_of(blk * tk, tk), tk), :]
        return jnp.concatenate([kblk, bias_ref[0]], axis=1)

    def stage(score_blk, score_bufs, value_blk, value_bufs, masked=False):
        if score_blk is not None:
            lhs = key_lhs(score_blk)
        if value_blk is not None:
            vblk = vt_ref[0, value_blk]
            dj = jnp.full((1, cw), value_blk - i, jnp.int32).astype(F32)
        for h, n in chunks:
            if score_blk is not None:
                score_chunk(lhs, *score_bufs, h, n, masked)
            if value_blk is not None:
                value_chunk(vblk, dj, *value_bufs, h, n)

    buf_a, buf_b = (sa_s, tma_s), (sb_s, tmb_s)
    stage(i, buf_a, None, None, masked=True)

    def pair(p, prev):
        stage(2 * p, buf_b, prev, buf_a)
        stage(2 * p + 1, buf_a, 2 * p, buf_b)
        return 2 * p + 1

    prev = lax.fori_loop(0, i // 2, pair, i)

    @pl.when(i % 2 == 1)
    def _():
        stage(i - 1, buf_b, prev, buf_a)
        stage(None, None, i - 1, buf_b)

    @pl.when(i % 2 == 0)
    def _():
        stage(None, None, prev, buf_a)

    lamv = lam_ref[...]
    lam = (jnp.exp(jnp.sum(lamv[0:1] * lamv[1:2], axis=1, keepdims=True))
           - jnp.exp(jnp.sum(lamv[2:3] * lamv[3:4], axis=1, keepdims=True)) + LAM_INIT)
    gain = jnp.concatenate([g_ref[...]] * (tq // LANES), axis=1)
    ys = []
    for h in range(2):
        a0 = acc_s[2 * h]
        a1 = acc_s[2 * h + 1]
        a = a0[0:DIFF_V] / a0[DIFF_V:DIFF_V + 1] - lam * (a1[0:DIFF_V] / a1[DIFF_V:DIFF_V + 1])
        ms = jnp.mean(a * a, axis=0, keepdims=True)
        ys.append(a * lax.rsqrt(ms + RMS_EPS) * gain * (1.0 - LAM_INIT))
    o_ref[0] = jnp.concatenate(ys, axis=0).T.astype(BF16)


def _attn(q, k, vt, bias, slope_tk, lam_vecs, gain):
    b, s, _ = q.shape
    tq, tk = Q_TILE, K_TILE
    pairs = DIFF_HEADS // 2
    return pl.pallas_call(
        _attn_kernel,
        out_shape=jax.ShapeDtypeStruct((b, s, BRANCH), BF16),
        grid=(b, pairs, s // tq),
        in_specs=[pl.BlockSpec((1, tq, LANES), lambda bi, hp, qi: (bi, qi, hp)),
                  pl.BlockSpec((1, s, LANES), lambda bi, hp, qi: (bi, 0, hp)),
                  pl.BlockSpec((1, s // tk, LANES, tk), lambda bi, hp, qi: (bi, 0, hp, 0)),
                  pl.BlockSpec((1, tk, LANES), lambda bi, hp, qi: (hp, 0, 0)),
                  pl.BlockSpec((1, 8, LANES), lambda bi, hp, qi: (hp, 0, 0)),
                  _const_spec((8, LANES), single_buffer=False),
                  _const_spec((DIFF_V, LANES), single_buffer=False)],
        out_specs=pl.BlockSpec((1, tq, LANES), lambda bi, hp, qi: (bi, qi, hp)),
        scratch_shapes=[pltpu.VMEM((2, 2 * LANES, 2 * tq), BF16),
                        pltpu.VMEM((8, tq), F32),
                        pltpu.VMEM((4, V_AUG, tq), F32),
                        pltpu.VMEM((2, tk, 2 * tq), F32),
                        pltpu.VMEM((2, tk, 2 * tq), F32),
                        pltpu.VMEM((8, 2 * tq), F32),
                        pltpu.VMEM((8, 2 * tq), F32)],
        compiler_params=pltpu.CompilerParams(dimension_semantics=("parallel", "parallel", "parallel"),
                                             vmem_limit_bytes=VMEM_LIMIT),
        name="attn",
    )(q, k, vt, bias, slope_tk, lam_vecs, gain)


def _epi_kernel(x_ref, att_ref, sz_ref, part_ref, gd_ref, lng_ref, lnb_ref, wb1_ref, wout_ref, og_ref, ob_ref,
                o_ref):
    h = _layer_norm(x_ref[0], lng_ref[...], lnb_ref[...])
    o_diff = (att_ref[0].astype(F32) * sz_ref[0].astype(F32)).astype(BF16)
    y_diff = jnp.dot(o_diff, wb1_ref[...], preferred_element_type=F32)
    merged = part_ref[0].astype(F32) + gd_ref[0].astype(F32) * y_diff
    out = jnp.dot(merged.astype(BF16), wout_ref[...], preferred_element_type=F32)
    o_ref[0] = _layer_norm(DEEPNORM_ALPHA * h + out, og_ref[...], ob_ref[...])


def _epi(x, att, sz, part, gd, ln_g, ln_b, wb1, w_out, out_g, out_b):
    b, s, _ = x.shape
    tt = TOKEN_TILE
    tok = lambda width: pl.BlockSpec((1, tt, width), lambda bi, si: (bi, si, 0))
    return pl.pallas_call(
        _epi_kernel,
        out_shape=jax.ShapeDtypeStruct((b, s, D_MODEL), F32),
        grid=(b, s // tt),
        in_specs=[tok(D_MODEL), tok(BRANCH), tok(BRANCH), tok(D_MODEL), tok(D_MODEL),
                  _const_spec((1, D_MODEL)), _const_spec((1, D_MODEL)),
                  _const_spec((BRANCH, D_MODEL)), _const_spec((D_MODEL, D_MODEL)),
                  _const_spec((1, D_MODEL)), _const_spec((1, D_MODEL))],
        out_specs=tok(D_MODEL),
        compiler_params=pltpu.CompilerParams(dimension_semantics=("parallel", "parallel"),
                                             vmem_limit_bytes=VMEM_LIMIT),
        name="epi",
    )(x, att, sz, part, gd, ln_g, ln_b, wb1, w_out, out_g, out_b)


def _alibi_tables():
    pos = np.arange(K_TILE)
    lo = (pos % 256).astype(np.float32)
    hi = (pos - pos % 256).astype(np.float32)
    slopes = [2.0 ** (-8.0 * (h + 1) / DIFF_HEADS) for h in range(DIFF_HEADS)]
    bias = np.zeros((DIFF_HEADS // 2, K_TILE, LANES), np.float32)
    slope_tk = np.zeros((DIFF_HEADS // 2, 8, LANES), np.float32)
    for hp in range(DIFF_HEADS // 2):
        for h in range(2):
            sl = slopes[2 * hp + h]
            for t in range(len(LOG2E_TERMS)):
                bias[hp, :, ALIBI_COLS * h + 2 * t] = sl * lo
                bias[hp, :, ALIBI_COLS * h + 2 * t + 1] = sl * hi
            slope_tk[hp, h, :] = sl * K_TILE * LOG2E
    return jnp.asarray(bias, BF16), jnp.asarray(slope_tk, F32)


def kernel(x, mem, ln_in_g, ln_in_b, w_in, b_gate, pool_w, pool_scale, lambda_q1, lambda_k1, lambda_q2, lambda_k2,
           diff_norm_g, w_mem_kv, w_branch, w_out, ln_out_g, ln_out_b):
    assert w_in.shape[0] == DEPTH == 1
    w = w_in[0]
    seg = lambda i: w[:, i * BRANCH:(i + 1) * BRANCH]
    w_main = jnp.concatenate([seg(0), seg(1), seg(2), seg(3), seg(5), seg(6), seg(7)], axis=1).astype(BF16)
    w_vt = seg(4).T.astype(BF16)
    w_g = w[:, 8 * BRANCH:].astype(BF16)
    row = lambda v: v.reshape(1, -1).astype(F32)

    mk, mv = _memkv(mem, w_mem_kv[0].astype(BF16))
    q, k, vt, sz, part, gd = _proj(
        x, row(ln_in_g), row(ln_in_b), w_main, w_vt, w_g, row(b_gate[0]), pool_w[0].astype(BF16),
        row(pool_scale[0]), mk, mv, w_branch[0, 0].astype(BF16), w_branch[0, 2].astype(BF16))

    bias, slope_tk = _alibi_tables()
    lam_vecs = jnp.concatenate([lambda_q1, lambda_k1, lambda_q2, lambda_k2], axis=0).astype(F32)
    lam_vecs = jnp.pad(lam_vecs, ((0, 4), (0, LANES - DIFF_QK)))
    gain = jnp.broadcast_to(diff_norm_g[0].astype(F32)[:, None], (DIFF_V, LANES))
    att = _attn(q, k, vt, bias, slope_tk, lam_vecs, gain)

    return _epi(x, att, sz, part, gd, row(ln_in_g), row(ln_in_b), w_branch[0, 1].astype(BF16),
                w_out[0].astype(BF16), row(ln_out_g[0]), row(ln_out_b[0]))
```

```python
import math
import struct

import jax
import jax.numpy as jnp
import numpy as np
from jax import lax
from jax.experimental import pallas as pl
from jax.experimental.pallas import tpu as pltpu

F32 = jnp.float32
BF16 = jnp.bfloat16

D_MODEL = 1024
BRANCH = 512
MEM_LEN = 256
POOL_WINDOWS = (2, 4, 8, 16)
POOL_GROUP = 128
DIFF_HEADS = 8
DIFF_QK = 32
DIFF_V = 64
MEM_HEADS = 4
MEM_HEAD_DIM = 128
LN_EPS = 1e-5
RMS_EPS = 1e-5
DEPTH = 1
DEEPNORM_ALPHA = (2.0 * DEPTH) ** 0.25
LAM_INIT = 0.8 - 0.6 * math.exp(-0.3 * 0)
LOG2E = math.log2(math.e)
DIFF_SCALE = DIFF_QK ** -0.5 * LOG2E


def _bf16_terms(x, n):
    terms = []
    for _ in range(n):
        bits = struct.unpack("<I", struct.pack("<f", x))[0]
        bits = (bits + 0x7FFF + ((bits >> 16) & 1)) & 0xFFFF0000
        t = struct.unpack("<f", struct.pack("<I", bits))[0]
        terms.append(t)
        x -= t
    return terms


LOG2E_TERMS = _bf16_terms(LOG2E, 3)
MEM_SCALE = MEM_HEAD_DIM ** -0.5
NEG_BIG = -1e30

TOKEN_TILE = 512
Q_TILE = 1024
K_TILE = TOKEN_TILE
CHUNK = 256
HALO = 32
LANES = 128
V_AUG = DIFF_V + 16
ALIBI_COLS = 2 * len(LOG2E_TERMS)
VMEM_LIMIT = 56 * 1024 * 1024

NT_DIMS = (((1,), (1,)), ((), ()))


def _layer_norm(x, g, b):
    mu = jnp.mean(x, axis=-1, keepdims=True)
    xc = x - mu
    var = jnp.mean(xc * xc, axis=-1, keepdims=True)
    return xc * lax.rsqrt(var + LN_EPS) * g + b


def _silu(x):
    return x * jax.nn.sigmoid(x)


def _const_spec(shape, single_buffer=True):
    n = len(shape)
    mode = pl.Buffered(1) if single_buffer else None
    return pl.BlockSpec(shape, lambda *_: (0,) * n, pipeline_mode=mode)


def _memkv_kernel(mem_ref, w_ref, mk_ref, mv_ref):
    kv = jnp.dot(mem_ref[0].astype(BF16), w_ref[...], preferred_element_type=F32)
    mk_ref[0] = kv[:, :BRANCH].astype(BF16)
    mv_ref[0] = kv[:, BRANCH:].astype(BF16)


def _memkv(mem, w_kv):
    b = mem.shape[0]
    return pl.pallas_call(
        _memkv_kernel,
        out_shape=(jax.ShapeDtypeStruct((b, MEM_LEN, BRANCH), BF16),
                   jax.ShapeDtypeStruct((b, MEM_LEN, BRANCH), BF16)),
        grid=(b,),
        in_specs=[pl.BlockSpec((1, MEM_LEN, D_MODEL), lambda i: (i, 0, 0)),
                  _const_spec((D_MODEL, 2 * BRANCH))],
        out_specs=(pl.BlockSpec((1, MEM_LEN, BRANCH), lambda i: (i, 0, 0)),
                   pl.BlockSpec((1, MEM_LEN, BRANCH), lambda i: (i, 0, 0))),
        name="memkv",
    )(mem, w_kv)


def _proj_kernel(x_ref, lng_ref, lnb_ref, wmain_ref, wvt_ref, wg_ref, bg_ref, poolw_ref, pscale_ref,
                 mk_ref, mv_ref, wb0_ref, wb2_ref,
                 q_ref, k_ref, vt_ref, sz_ref, part_ref, gd_ref,
                 e1_ref, e2_ref, e4_ref, e8_ref):
    tt = TOKEN_TILE
    s_idx = pl.program_id(1)
    h = _layer_norm(x_ref[0], lng_ref[...], lnb_ref[...])
    hb = h.astype(BF16)

    def proj(seg):
        return jnp.dot(hb, wmain_ref[:, seg * BRANCH:(seg + 1) * BRANCH], preferred_element_type=F32)

    q_ref[0] = (proj(2) * DIFF_SCALE).astype(BF16)
    k_ref[0] = proj(3).astype(BF16)
    vt = lax.dot_general(wvt_ref[...], hb, NT_DIMS, preferred_element_type=F32)
    vt_ref[0, 0] = vt.astype(BF16)
    sz_ref[0] = _silu(proj(4)).astype(BF16)

    u = proj(0)

    @pl.when(s_idx == 0)
    def _():
        e1_ref[0:HALO, :] = jnp.zeros((HALO, BRANCH), F32)

    e1_ref[HALO:HALO + tt, :] = u
    n = HALO + tt
    e2_ref[8:n, :] = e1_ref[8:n, :] + e1_ref[7:n - 1, :]
    e4_ref[16:n, :] = e2_ref[16:n, POOL_GROUP:] + e2_ref[14:n - 2, POOL_GROUP:]
    e8_ref[24:n, :] = e4_ref[24:n, POOL_GROUP:] + e4_ref[20:n - 4, POOL_GROUP:]
    s16 = e8_ref[HALO:n, POOL_GROUP:] + e8_ref[HALO - 8:n - 8, POOL_GROUP:]
    wsum = (e2_ref[HALO:n, 0:POOL_GROUP], e4_ref[HALO:n, 0:POOL_GROUP], e8_ref[HALO:n, 0:POOL_GROUP], s16)
    e1_ref[0:HALO, :] = e1_ref[tt:tt + HALO, :]

    t_pos = s_idx * tt + lax.broadcasted_iota(jnp.int32, (tt, POOL_GROUP), 0)
    pooled = []
    for g, w in enumerate(POOL_WINDOWS):
        cnt = jnp.minimum(t_pos + 1, w).astype(F32)
        pg = wsum[g] / cnt - u[:, g * POOL_GROUP:(g + 1) * POOL_GROUP]
        pooled.append(jnp.dot(pg.astype(BF16), poolw_ref[g], preferred_element_type=F32))
    o_pool = jnp.concatenate(pooled, axis=1) * pscale_ref[...] * _silu(proj(1))

    mq = (proj(5) * MEM_SCALE).astype(BF16)
    heads = []
    for hh in range(MEM_HEADS):
        sl = slice(hh * MEM_HEAD_DIM, (hh + 1) * MEM_HEAD_DIM)
        s = lax.dot_general(mq[:, sl], mk_ref[0, :, sl], NT_DIMS, preferred_element_type=F32)
        p = jnp.exp(s - jnp.max(s, axis=-1, keepdims=True))
        l = jnp.sum(p, axis=-1, keepdims=True)
        heads.append(jnp.dot(p.astype(BF16), mv_ref[0, :, sl], preferred_element_type=F32) / l)
    o_mem = jnp.concatenate(heads, axis=1) * _silu(proj(6))

    def gate(nb):
        sl = slice(nb * D_MODEL, (nb + 1) * D_MODEL)
        return jax.nn.sigmoid(jnp.dot(hb, wg_ref[:, sl], preferred_element_type=F32) + bg_ref[:, sl])

    y_pool = jnp.dot(o_pool.astype(BF16), wb0_ref[...], preferred_element_type=F32)
    y_mem = jnp.dot(o_mem.astype(BF16), wb2_ref[...], preferred_element_type=F32)
    part_ref[0] = (gate(0) * y_pool + gate(2) * y_mem).astype(BF16)
    gd_ref[0] = gate(1).astype(BF16)


def _proj(x, ln_g, ln_b, w_main, w_vt, w_g, b_g, pool_w, pool_scale, mk, mv, wb0, wb2):
    b, s, _ = x.shape
    tt = TOKEN_TILE
    ns = s // tt
    tok = lambda width: pl.BlockSpec((1, tt, width), lambda bi, si: (bi, si, 0))
    per_batch = pl.BlockSpec((1, MEM_LEN, BRANCH), lambda bi, si: (bi, 0, 0))
    return pl.pallas_call(
        _proj_kernel,
        out_shape=(jax.ShapeDtypeStruct((b, s, BRANCH), BF16),
                   jax.ShapeDtypeStruct((b, s, BRANCH), BF16),
                   jax.ShapeDtypeStruct((b, ns, BRANCH, tt), BF16),
                   jax.ShapeDtypeStruct((b, s, BRANCH), BF16),
                   jax.ShapeDtypeStruct((b, s, D_MODEL), BF16),
                   jax.ShapeDtypeStruct((b, s, D_MODEL), BF16)),
        grid=(b, ns),
        in_specs=[tok(D_MODEL),
                  _const_spec((1, D_MODEL)), _const_spec((1, D_MODEL)),
                  _const_spec((D_MODEL, 7 * BRANCH)), _const_spec((BRANCH, D_MODEL)),
                  _const_spec((D_MODEL, 3 * D_MODEL)), _const_spec((1, 3 * D_MODEL)),
                  _const_spec((len(POOL_WINDOWS), POOL_GROUP, POOL_GROUP)), _const_spec((1, BRANCH)),
                  per_batch, per_batch,
                  _const_spec((BRANCH, D_MODEL)), _const_spec((BRANCH, D_MODEL))],
        out_specs=(tok(BRANCH), tok(BRANCH),
                   pl.BlockSpec((1, 1, BRANCH, tt), lambda bi, si: (bi, si, 0, 0)),
                   tok(BRANCH), tok(D_MODEL), tok(D_MODEL)),
        scratch_shapes=[pltpu.VMEM((HALO + tt, BRANCH), F32),
                        pltpu.VMEM((HALO + tt, BRANCH), F32),
                        pltpu.VMEM((HALO + tt, BRANCH - POOL_GROUP), F32),
                        pltpu.VMEM((HALO + tt, BRANCH - 2 * POOL_GROUP), F32)],
        compiler_params=pltpu.CompilerParams(dimension_semantics=("arbitrary", "arbitrary"),
                                             vmem_limit_bytes=VMEM_LIMIT),
        name="proj",
    )(x, ln_g, ln_b, w_main, w_vt, w_g, b_g, pool_w, pool_scale, mk, mv, wb0, wb2)


def _attn_kernel(q_ref, k_ref, vt_ref, bias_ref, slope_ref, lam_ref, g_ref, o_ref, qt_s, m_s, acc_s,
                 sa_s, sb_s, tma_s, tmb_s):
    tq, tk = Q_TILE, K_TILE
    i = pl.program_id(2)

    qt = q_ref[0].astype(F32).T.astype(BF16)
    row = lax.broadcasted_iota(jnp.int32, (16, 2 * tq), 0)
    for h in range(2):
        qt_s[h] = jnp.zeros((2 * LANES, 2 * tq), BF16)
        base = 2 * DIFF_QK * h
        qt_s[h, base:base + DIFF_QK, 0:tq] = qt[base:base + DIFF_QK]
        qt_s[h, base + DIFF_QK:base + 2 * DIFF_QK, tq:2 * tq] = qt[base + DIFF_QK:base + 2 * DIFF_QK]
        alibi_rows = jnp.zeros((16, 2 * tq), F32)
        for t, c in enumerate(LOG2E_TERMS):
            r0 = ALIBI_COLS * h + 2 * t
            alibi_rows = jnp.where((row == r0) | (row == r0 + 1), c, alibi_rows)
        qt_s[h, LANES:LANES + 16, :] = alibi_rows.astype(BF16)
    m_s[...] = jnp.full(m_s.shape, NEG_BIG, F32)
    acc_s[...] = jnp.zeros(acc_s.shape, F32)

    ones_rows = jnp.where(lax.broadcasted_iota(jnp.int32, (16, tk), 0) == 0, 1.0, 0.0).astype(BF16)
    slope_tk = [jnp.concatenate([slope_ref[0, h:h + 1, :]] * (tq // LANES), axis=1) for h in range(2)]

    cw = CHUNK
    chunks = [(h, n) for h in range(2) for n in range(2 * tq // cw)]

    def q_start(n):
        return (n * cw) % tq

    def score_chunk(lhs, s_buf, tm_buf, h, n, key_start):
        cols = slice(n * cw, (n + 1) * cw)
        st = jnp.dot(lhs, qt_s[h, :, cols], preferred_element_type=F32)
        if key_start is not None and key_start + tk - 1 > q_start(n):
            kk = lax.broadcasted_iota(jnp.int32, (tk, cw), 0) + key_start
            qq = lax.broadcasted_iota(jnp.int32, (tk, cw), 1) + q_start(n)
            st = jnp.where(kk <= qq, st, NEG_BIG)
        s_buf[h, :, cols] = st
        tm_buf[h:h + 1, cols] = jnp.max(st, axis=0, keepdims=True)

    def value_chunk(vblk, dj, s_buf, tm_buf, h, n):
        idx = 2 * h + (n * cw) // tq
        cols = slice(n * cw, (n + 1) * cw)
        qcols = slice(q_start(n), q_start(n) + cw)
        off = slope_tk[h][:, qcols] * dj
        vaug = jnp.concatenate([vblk[DIFF_V * h:DIFF_V * (h + 1)], ones_rows], axis=0)
        m_old = m_s[idx:idx + 1, qcols]
        m_new = jnp.maximum(m_old, tm_buf[h:h + 1, cols] + off)
        alpha = jnp.exp2(m_old - m_new)
        p = jnp.exp2(s_buf[h, :, cols] - (m_new - off)).astype(BF16)
        pv = jnp.dot(vaug, p, preferred_element_type=F32)
        acc_s[idx, :, qcols] = alpha * acc_s[idx, :, qcols] + pv
        m_s[idx:idx + 1, qcols] = m_new

    def key_lhs(blk):
        kblk = k_ref[0, pl.ds(pl.multiple_of(blk * tk, tk), tk), :]
        return jnp.concatenate([kblk, bias_ref[0]], axis=1)

    def stage(score=None, value=None):
        if score is not None:
            s_blk, s_bufs, key_start, s_chunks = score
            lhs = key_lhs(s_blk)
        if value is not None:
            v_blk, v_bufs, v_chunks = value
            vblk = vt_ref[0, v_blk]
            dj = jnp.full((1, cw), v_blk - first_diag, jnp.int32).astype(F32)
        for c in chunks:
            if score is not None and c in s_chunks:
                score_chunk(lhs, *s_bufs, *c, key_start)
            if value is not None and c in v_chunks:
                value_chunk(vblk, dj, *v_bufs, *c)

    buf_a, buf_b = (sa_s, tma_s), (sb_s, tmb_s)
    first_diag = (tq // tk) * i
    late_chunks = [c for c in chunks if q_start(c[1]) >= tk]
    stage(score=(first_diag + 1, buf_a, tk, late_chunks))
    stage(score=(first_diag, buf_b, 0, chunks), value=(first_diag + 1, buf_a, late_chunks))

    def pair(p, prev):
        stage(score=(2 * p, buf_a, None, chunks), value=(prev, buf_b, chunks))
        stage(score=(2 * p + 1, buf_b, None, chunks), value=(2 * p, buf_a, chunks))
        return 2 * p + 1

    last = lax.fori_loop(0, i, pair, first_diag)
    stage(value=(last, buf_b, chunks))

    lamv = lam_ref[...]
    lam = (jnp.exp(jnp.sum(lamv[0:1] * lamv[1:2], axis=1, keepdims=True))
           - jnp.exp(jnp.sum(lamv[2:3] * lamv[3:4], axis=1, keepdims=True)) + LAM_INIT)
    gain = jnp.concatenate([g_ref[...]] * (tq // LANES), axis=1)
    ys = []
    for h in range(2):
        a0 = acc_s[2 * h]
        a1 = acc_s[2 * h + 1]
        a = a0[0:DIFF_V] / a0[DIFF_V:DIFF_V + 1] - lam * (a1[0:DIFF_V] / a1[DIFF_V:DIFF_V + 1])
        ms = jnp.mean(a * a, axis=0, keepdims=True)
        ys.append(a * lax.rsqrt(ms + RMS_EPS) * gain * (1.0 - LAM_INIT))
    o_ref[0] = jnp.concatenate(ys, axis=0).T.astype(BF16)


def _attn(q, k, vt, bias, slope_tk, lam_vecs, gain):
    b, s, _ = q.shape
    tq, tk = Q_TILE, K_TILE
    pairs = DIFF_HEADS // 2
    return pl.pallas_call(
        _attn_kernel,
        out_shape=jax.ShapeDtypeStruct((b, s, BRANCH), BF16),
        grid=(b, pairs, s // tq),
        in_specs=[pl.BlockSpec((1, tq, LANES), lambda bi, hp, qi: (bi, qi, hp)),
                  pl.BlockSpec((1, s, LANES), lambda bi, hp, qi: (bi, 0, hp)),
                  pl.BlockSpec((1, s // tk, LANES, tk), lambda bi, hp, qi: (bi, 0, hp, 0)),
                  pl.BlockSpec((1, tk, LANES), lambda bi, hp, qi: (hp, 0, 0)),
                  pl.BlockSpec((1, 8, LANES), lambda bi, hp, qi: (hp, 0, 0)),
                  _const_spec((8, LANES), single_buffer=False),
                  _const_spec((DIFF_V, LANES), single_buffer=False)],
        out_specs=pl.BlockSpec((1, tq, LANES), lambda bi, hp, qi: (bi, qi, hp)),
        scratch_shapes=[pltpu.VMEM((2, 2 * LANES, 2 * tq), BF16),
                        pltpu.VMEM((8, tq), F32),
                        pltpu.VMEM((4, V_AUG, tq), F32),
                        pltpu.VMEM((2, tk, 2 * tq), F32),
                        pltpu.VMEM((2, tk, 2 * tq), F32),
                        pltpu.VMEM((8, 2 * tq), F32),
                        pltpu.VMEM((8, 2 * tq), F32)],
        compiler_params=pltpu.CompilerParams(dimension_semantics=("parallel", "parallel", "parallel"),
                                             vmem_limit_bytes=VMEM_LIMIT),
        name="attn",
    )(q, k, vt, bias, slope_tk, lam_vecs, gain)


def _epi_kernel(x_ref, att_ref, sz_ref, part_ref, gd_ref, lng_ref, lnb_ref, wb1_ref, wout_ref, og_ref, ob_ref,
                o_ref):
    h = _layer_norm(x_ref[0], lng_ref[...], lnb_ref[...])
    o_diff = (att_ref[0].astype(F32) * sz_ref[0].astype(F32)).astype(BF16)
    y_diff = jnp.dot(o_diff, wb1_ref[...], preferred_element_type=F32)
    merged = part_ref[0].astype(F32) + gd_ref[0].astype(F32) * y_diff
    out = jnp.dot(merged.astype(BF16), wout_ref[...], preferred_element_type=F32)
    o_ref[0] = _layer_norm(DEEPNORM_ALPHA * h + out, og_ref[...], ob_ref[...])


def _epi(x, att, sz, part, gd, ln_g, ln_b, wb1, w_out, out_g, out_b):
    b, s, _ = x.shape
    tt = TOKEN_TILE
    tok = lambda width: pl.BlockSpec((1, tt, width), lambda bi, si: (bi, si, 0))
    return pl.pallas_call(
        _epi_kernel,
        out_shape=jax.ShapeDtypeStruct((b, s, D_MODEL), F32),
        grid=(b, s // tt),
        in_specs=[tok(D_MODEL), tok(BRANCH), tok(BRANCH), tok(D_MODEL), tok(D_MODEL),
                  _const_spec((1, D_MODEL)), _const_spec((1, D_MODEL)),
                  _const_spec((BRANCH, D_MODEL)), _const_spec((D_MODEL, D_MODEL)),
                  _const_spec((1, D_MODEL)), _const_spec((1, D_MODEL))],
        out_specs=tok(D_MODEL),
        compiler_params=pltpu.CompilerParams(dimension_semantics=("parallel", "parallel"),
                                             vmem_limit_bytes=VMEM_LIMIT),
        name="epi",
    )(x, att, sz, part, gd, ln_g, ln_b, wb1, w_out, out_g, out_b)


def _alibi_tables():
    pos = np.arange(K_TILE)
    lo = (pos % 256).astype(np.float32)
    hi = (pos - pos % 256).astype(np.float32)
    slopes = [2.0 ** (-8.0 * (h + 1) / DIFF_HEADS) for h in range(DIFF_HEADS)]
    bias = np.zeros((DIFF_HEADS // 2, K_TILE, LANES), np.float32)
    slope_tk = np.zeros((DIFF_HEADS // 2, 8, LANES), np.float32)
    for hp in range(DIFF_HEADS // 2):
        for h in range(2):
            sl = slopes[2 * hp + h]
            for t in range(len(LOG2E_TERMS)):
                bias[hp, :, ALIBI_COLS * h + 2 * t] = sl * lo
                bias[hp, :, ALIBI_COLS * h + 2 * t + 1] = sl * hi
            slope_tk[hp, h, :] = sl * K_TILE * LOG2E
    return jnp.asarray(bias, BF16), jnp.asarray(slope_tk, F32)


def kernel(x, mem, ln_in_g, ln_in_b, w_in, b_gate, pool_w, pool_scale, lambda_q1, lambda_k1, lambda_q2, lambda_k2,
           diff_norm_g, w_mem_kv, w_branch, w_out, ln_out_g, ln_out_b):
    assert w_in.shape[0] == DEPTH == 1
    w = w_in[0]
    seg = lambda i: w[:, i * BRANCH:(i + 1) * BRANCH]
    w_main = jnp.concatenate([seg(0), seg(1), seg(2), seg(3), seg(5), seg(6), seg(7)], axis=1).astype(BF16)
    w_vt = seg(4).T.astype(BF16)
    w_g = w[:, 8 * BRANCH:].astype(BF16)
    row = lambda v: v.reshape(1, -1).astype(F32)

    mk, mv = _memkv(mem, w_mem_kv[0].astype(BF16))
    q, k, vt, sz, part, gd = _proj(
        x, row(ln_in_g), row(ln_in_b), w_main, w_vt, w_g, row(b_gate[0]), pool_w[0].astype(BF16),
        row(pool_scale[0]), mk, mv, w_branch[0, 0].astype(BF16), w_branch[0, 2].astype(BF16))

    bias, slope_tk = _alibi_tables()
    lam_vecs = jnp.concatenate([lambda_q1, lambda_k1, lambda_q2, lambda_k2], axis=0).astype(F32)
    lam_vecs = jnp.pad(lam_vecs, ((0, 4), (0, LANES - DIFF_QK)))
    gain = jnp.broadcast_to(diff_norm_g[0].astype(F32)[:, None], (DIFF_V, LANES))
    att = _attn(q, k, vt, bias, slope_tk, lam_vecs, gain)

    return _epi(x, att, sz, part, gd, row(ln_in_g), row(ln_in_b), w_branch[0, 1].astype(BF16),
                w_out[0].astype(BF16), row(ln_out_g[0]), row(ln_out_b[0]))
```

```python
import math
import struct

import jax
import jax.numpy as jnp
import numpy as np
from jax import lax
from jax.experimental import pallas as pl
from jax.experimental.pallas import tpu as pltpu

F32 = jnp.float32
BF16 = jnp.bfloat16

D_MODEL = 1024
BRANCH = 512
MEM_LEN = 256
POOL_WINDOWS = (2, 4, 8, 16)
POOL_GROUP = 128
DIFF_HEADS = 8
DIFF_QK = 32
DIFF_V = 64
MEM_HEADS = 4
MEM_HEAD_DIM = 128
LN_EPS = 1e-5
RMS_EPS = 1e-5
DEPTH = 1
DEEPNORM_ALPHA = (2.0 * DEPTH) ** 0.25
LAM_INIT = 0.8 - 0.6 * math.exp(-0.3 * 0)
LOG2E = math.log2(math.e)
DIFF_SCALE = DIFF_QK ** -0.5 * LOG2E


def _bf16_terms(x, n):
    terms = []
    for _ in range(n):
        bits = struct.unpack("<I", struct.pack("<f", x))[0]
        bits = (bits + 0x7FFF + ((bits >> 16) & 1)) & 0xFFFF0000
        t = struct.unpack("<f", struct.pack("<I", bits))[0]
        terms.append(t)
        x -= t
    return terms


LOG2E_TERMS = _bf16_terms(LOG2E, 3)
MEM_SCALE = MEM_HEAD_DIM ** -0.5
NEG_BIG = -1e30
ZERO_PROB_EXPONENT = 160.0
BOUND_SLACK_REL = 1.001
BOUND_SLACK_ABS = 1.0

TOKEN_TILE = 512
Q_TILE = 1024
K_TILE = TOKEN_TILE
CHUNK = 256
HALO = 32
LANES = 128
V_AUG = DIFF_V + 16
ALIBI_COLS = 2 * len(LOG2E_TERMS)
VMEM_LIMIT = 56 * 1024 * 1024

NT_DIMS = (((1,), (1,)), ((), ()))


def _layer_norm(x, g, b):
    mu = jnp.mean(x, axis=-1, keepdims=True)
    xc = x - mu
    var = jnp.mean(xc * xc, axis=-1, keepdims=True)
    return xc * lax.rsqrt(var + LN_EPS) * g + b


def _silu(x):
    return x * jax.nn.sigmoid(x)


def _const_spec(shape, single_buffer=True):
    n = len(shape)
    mode = pl.Buffered(1) if single_buffer else None
    return pl.BlockSpec(shape, lambda *_: (0,) * n, pipeline_mode=mode)


def _memkv_kernel(mem_ref, w_ref, mk_ref, mv_ref):
    kv = jnp.dot(mem_ref[0].astype(BF16), w_ref[...], preferred_element_type=F32)
    mk_ref[0] = kv[:, :BRANCH].astype(BF16)
    mv_ref[0] = kv[:, BRANCH:].astype(BF16)


def _memkv(mem, w_kv):
    b = mem.shape[0]
    return pl.pallas_call(
        _memkv_kernel,
        out_shape=(jax.ShapeDtypeStruct((b, MEM_LEN, BRANCH), BF16),
                   jax.ShapeDtypeStruct((b, MEM_LEN, BRANCH), BF16)),
        grid=(b,),
        in_specs=[pl.BlockSpec((1, MEM_LEN, D_MODEL), lambda i: (i, 0, 0)),
                  _const_spec((D_MODEL, 2 * BRANCH))],
        out_specs=(pl.BlockSpec((1, MEM_LEN, BRANCH), lambda i: (i, 0, 0)),
                   pl.BlockSpec((1, MEM_LEN, BRANCH), lambda i: (i, 0, 0))),
        name="memkv",
    )(mem, w_kv)


def _proj_kernel(x_ref, lng_ref, lnb_ref, wmain_ref, wvt_ref, wg_ref, bg_ref, poolw_ref, pscale_ref,
                 mk_ref, mv_ref, wb0_ref, wb2_ref,
                 q_ref, k_ref, vt_ref, sz_ref, part_ref, gd_ref, kabs_ref,
                 e1_ref, e2_ref, e4_ref, e8_ref):
    tt = TOKEN_TILE
    s_idx = pl.program_id(1)
    h = _layer_norm(x_ref[0], lng_ref[...], lnb_ref[...])
    hb = h.astype(BF16)

    def proj(seg):
        return jnp.dot(hb, wmain_ref[:, seg * BRANCH:(seg + 1) * BRANCH], preferred_element_type=F32)

    q_ref[0] = (proj(2) * DIFF_SCALE).astype(BF16)
    kb = proj(3).astype(BF16)
    k_ref[0] = kb
    kabs_ref[0, 0] = jnp.broadcast_to(jnp.max(jnp.abs(kb.astype(F32)), axis=0, keepdims=True), (8, BRANCH))
    vt = lax.dot_general(wvt_ref[...], hb, NT_DIMS, preferred_element_type=F32)
    vt_ref[0, 0] = vt.astype(BF16)
    sz_ref[0] = _silu(proj(4)).astype(BF16)

    u = proj(0)

    @pl.when(s_idx == 0)
    def _():
        e1_ref[0:HALO, :] = jnp.zeros((HALO, BRANCH), F32)

    e1_ref[HALO:HALO + tt, :] = u
    n = HALO + tt
    e2_ref[8:n, :] = e1_ref[8:n, :] + e1_ref[7:n - 1, :]
    e4_ref[16:n, :] = e2_ref[16:n, POOL_GROUP:] + e2_ref[14:n - 2, POOL_GROUP:]
    e8_ref[24:n, :] = e4_ref[24:n, POOL_GROUP:] + e4_ref[20:n - 4, POOL_GROUP:]
    s16 = e8_ref[HALO:n, POOL_GROUP:] + e8_ref[HALO - 8:n - 8, POOL_GROUP:]
    wsum = (e2_ref[HALO:n, 0:POOL_GROUP], e4_ref[HALO:n, 0:POOL_GROUP], e8_ref[HALO:n, 0:POOL_GROUP], s16)
    e1_ref[0:HALO, :] = e1_ref[tt:tt + HALO, :]

    t_pos = s_idx * tt + lax.broadcasted_iota(jnp.int32, (tt, POOL_GROUP), 0)
    pooled = []
    for g, w in enumerate(POOL_WINDOWS):
        cnt = jnp.minimum(t_pos + 1, w).astype(F32)
        pg = wsum[g] / cnt - u[:, g * POOL_GROUP:(g + 1) * POOL_GROUP]
        pooled.append(jnp.dot(pg.astype(BF16), poolw_ref[g], preferred_element_type=F32))
    o_pool = jnp.concatenate(pooled, axis=1) * pscale_ref[...] * _silu(proj(1))

    mq = (proj(5) * MEM_SCALE).astype(BF16)
    heads = []
    for hh in range(MEM_HEADS):
        sl = slice(hh * MEM_HEAD_DIM, (hh + 1) * MEM_HEAD_DIM)
        s = lax.dot_general(mq[:, sl], mk_ref[0, :, sl], NT_DIMS, preferred_element_type=F32)
        p = jnp.exp(s - jnp.max(s, axis=-1, keepdims=True))
        l = jnp.sum(p, axis=-1, keepdims=True)
        heads.append(jnp.dot(p.astype(BF16), mv_ref[0, :, sl], preferred_element_type=F32) / l)
    o_mem = jnp.concatenate(heads, axis=1) * _silu(proj(6))

    def gate(nb):
        sl = slice(nb * D_MODEL, (nb + 1) * D_MODEL)
        return jax.nn.sigmoid(jnp.dot(hb, wg_ref[:, sl], preferred_element_type=F32) + bg_ref[:, sl])

    y_pool = jnp.dot(o_pool.astype(BF16), wb0_ref[...], preferred_element_type=F32)
    y_mem = jnp.dot(o_mem.astype(BF16), wb2_ref[...], preferred_element_type=F32)
    part_ref[0] = (gate(0) * y_pool + gate(2) * y_mem).astype(BF16)
    gd_ref[0] = gate(1).astype(BF16)


def _proj(x, ln_g, ln_b, w_main, w_vt, w_g, b_g, pool_w, pool_scale, mk, mv, wb0, wb2):
    b, s, _ = x.shape
    tt = TOKEN_TILE
    ns = s // tt
    tok = lambda width: pl.BlockSpec((1, tt, width), lambda bi, si: (bi, si, 0))
    per_batch = pl.BlockSpec((1, MEM_LEN, BRANCH), lambda bi, si: (bi, 0, 0))
    return pl.pallas_call(
        _proj_kernel,
        out_shape=(jax.ShapeDtypeStruct((b, s, BRANCH), BF16),
                   jax.ShapeDtypeStruct((b, s, BRANCH), BF16),
                   jax.ShapeDtypeStruct((b, ns, BRANCH, tt), BF16),
                   jax.ShapeDtypeStruct((b, s, BRANCH), BF16),
                   jax.ShapeDtypeStruct((b, s, D_MODEL), BF16),
                   jax.ShapeDtypeStruct((b, s, D_MODEL), BF16),
                   jax.ShapeDtypeStruct((b, ns, 8, BRANCH), F32)),
        grid=(b, ns),
        in_specs=[tok(D_MODEL),
                  _const_spec((1, D_MODEL)), _const_spec((1, D_MODEL)),
                  _const_spec((D_MODEL, 7 * BRANCH)), _const_spec((BRANCH, D_MODEL)),
                  _const_spec((D_MODEL, 3 * D_MODEL)), _const_spec((1, 3 * D_MODEL)),
                  _const_spec((len(POOL_WINDOWS), POOL_GROUP, POOL_GROUP)), _const_spec((1, BRANCH)),
                  per_batch, per_batch,
                  _const_spec((BRANCH, D_MODEL)), _const_spec((BRANCH, D_MODEL))],
        out_specs=(tok(BRANCH), tok(BRANCH),
                   pl.BlockSpec((1, 1, BRANCH, tt), lambda bi, si: (bi, si, 0, 0)),
                   tok(BRANCH), tok(D_MODEL), tok(D_MODEL),
                   pl.BlockSpec((1, 1, 8, BRANCH), lambda bi, si: (bi, si, 0, 0))),
        scratch_shapes=[pltpu.VMEM((HALO + tt, BRANCH), F32),
                        pltpu.VMEM((HALO + tt, BRANCH), F32),
                        pltpu.VMEM((HALO + tt, BRANCH - POOL_GROUP), F32),
                        pltpu.VMEM((HALO + tt, BRANCH - 2 * POOL_GROUP), F32)],
        compiler_params=pltpu.CompilerParams(dimension_semantics=("arbitrary", "arbitrary"),
                                             vmem_limit_bytes=VMEM_LIMIT),
        name="proj",
    )(x, ln_g, ln_b, w_main, w_vt, w_g, b_g, pool_w, pool_scale, mk, mv, wb0, wb2)


def _attn_kernel(q_ref, k_ref, vt_ref, kabs_ref, bias_ref, slope_ref, lam_ref, g_ref, o_ref, qt_s, m_s, acc_s,
                 sa_s, sb_s, tma_s, tmb_s):
    tq, tk = Q_TILE, K_TILE
    i = pl.program_id(2)

    qt32 = q_ref[0].astype(F32).T
    qt = qt32.astype(BF16)
    kcol = jnp.max(kabs_ref[0], axis=0).T[:, 0:1]
    qk_bound = jnp.abs(qt32) * kcol
    score_bound = [jnp.sum(qk_bound[DIFF_QK * idx:DIFF_QK * (idx + 1)], axis=0, keepdims=True)
                   for idx in range(4)]
    row = lax.broadcasted_iota(jnp.int32, (16, 2 * tq), 0)
    for h in range(2):
        qt_s[h] = jnp.zeros((2 * LANES, 2 * tq), BF16)
        base = 2 * DIFF_QK * h
        qt_s[h, base:base + DIFF_QK, 0:tq] = qt[base:base + DIFF_QK]
        qt_s[h, base + DIFF_QK:base + 2 * DIFF_QK, tq:2 * tq] = qt[base + DIFF_QK:base + 2 * DIFF_QK]
        alibi_rows = jnp.zeros((16, 2 * tq), F32)
        for t, c in enumerate(LOG2E_TERMS):
            r0 = ALIBI_COLS * h + 2 * t
            alibi_rows = jnp.where((row == r0) | (row == r0 + 1), c, alibi_rows)
        qt_s[h, LANES:LANES + 16, :] = alibi_rows.astype(BF16)
    m_s[...] = jnp.full(m_s.shape, NEG_BIG, F32)
    acc_s[...] = jnp.zeros(acc_s.shape, F32)

    ones_rows = jnp.where(lax.broadcasted_iota(jnp.int32, (16, tk), 0) == 0, 1.0, 0.0).astype(BF16)
    slope_tk = [jnp.concatenate([slope_ref[0, h:h + 1, :]] * (tq // LANES), axis=1) for h in range(2)]

    cw = CHUNK
    chunks = [(h, n) for h in range(2) for n in range(2 * tq // cw)]

    def q_start(n):
        return (n * cw) % tq

    def score_chunk(lhs, s_buf, tm_buf, h, n, key_start):
        cols = slice(n * cw, (n + 1) * cw)
        st = jnp.dot(lhs, qt_s[h, :, cols], preferred_element_type=F32)
        if key_start is not None and key_start + tk - 1 > q_start(n):
            kk = lax.broadcasted_iota(jnp.int32, (tk, cw), 0) + key_start
            qq = lax.broadcasted_iota(jnp.int32, (tk, cw), 1) + q_start(n)
            st = jnp.where(kk <= qq, st, NEG_BIG)
        s_buf[h, :, cols] = st
        tm_buf[h:h + 1, cols] = jnp.max(st, axis=0, keepdims=True)

    def value_chunk(vblk, dj, s_buf, tm_buf, h, n):
        idx = 2 * h + (n * cw) // tq
        cols = slice(n * cw, (n + 1) * cw)
        qcols = slice(q_start(n), q_start(n) + cw)
        off = slope_tk[h][:, qcols] * dj
        vaug = jnp.concatenate([vblk[DIFF_V * h:DIFF_V * (h + 1)], ones_rows], axis=0)
        m_old = m_s[idx:idx + 1, qcols]
        m_new = jnp.maximum(m_old, tm_buf[h:h + 1, cols] + off)
        alpha = jnp.exp2(m_old - m_new)
        p = jnp.exp2(s_buf[h, :, cols] - (m_new - off)).astype(BF16)
        pv = jnp.dot(vaug, p, preferred_element_type=F32)
        acc_s[idx, :, qcols] = alpha * acc_s[idx, :, qcols] + pv
        m_s[idx:idx + 1, qcols] = m_new

    def key_lhs(blk):
        kblk = k_ref[0, pl.ds(pl.multiple_of(blk * tk, tk), tk), :]
        return jnp.concatenate([kblk, bias_ref[0]], axis=1)

    def stage(score=None, value=None):
        if score is not None:
            s_blk, s_bufs, key_start, s_chunks = score
            lhs = key_lhs(s_blk)
        if value is not None:
            v_blk, v_bufs, v_chunks = value
            vblk = vt_ref[0, v_blk]
            dj = jnp.full((1, cw), v_blk - first_diag, jnp.int32).astype(F32)
        for c in chunks:
            if score is not None and c in s_chunks:
                score_chunk(lhs, *s_bufs, *c, key_start)
            if value is not None and c in v_chunks:
                value_chunk(vblk, dj, *v_bufs, *c)

    buf_a, buf_b = (sa_s, tma_s), (sb_s, tmb_s)
    first_diag = (tq // tk) * i
    late_chunks = [c for c in chunks if q_start(c[1]) >= tk]
    stage(score=(first_diag + 1, buf_a, tk, late_chunks))
    stage(score=(first_diag, buf_b, 0, chunks), value=(first_diag + 1, buf_a, late_chunks))

    def pair(p, prev):
        stage(score=(2 * p, buf_a, None, chunks), value=(prev, buf_b, chunks))
        stage(score=(2 * p + 1, buf_b, None, chunks), value=(2 * p, buf_a, chunks))
        return 2 * p + 1

    tiles_needed = None
    for h in range(2):
        gap = None
        for mm in range(2):
            idx = 2 * h + mm
            m_now = jnp.maximum(m_s[idx:idx + 1, :], tmb_s[h:h + 1, mm * tq:(mm + 1) * tq])
            g = jnp.max(score_bound[idx] * BOUND_SLACK_REL + BOUND_SLACK_ABS - m_now, axis=1, keepdims=True)
            gap = g if gap is None else jnp.maximum(gap, g)
        need_h = (gap + ZERO_PROB_EXPONENT) / slope_ref[0, h:h + 1, 0:1] + (tk - 1) / tk
        tiles_needed = need_h if tiles_needed is None else jnp.maximum(tiles_needed, need_h)
    n_full = jnp.full((1, 1), first_diag, jnp.int32)
    first_tile = jnp.ceil(n_full.astype(F32) - tiles_needed - 0.01).astype(jnp.int32)
    first_pair = jnp.max(jnp.clip(first_tile, 0, n_full)) // 2

    last = lax.fori_loop(first_pair, i, pair, first_diag)
    stage(value=(last, buf_b, chunks))

    lamv = lam_ref[...]
    lam = (jnp.exp(jnp.sum(lamv[0:1] * lamv[1:2], axis=1, keepdims=True))
           - jnp.exp(jnp.sum(lamv[2:3] * lamv[3:4], axis=1, keepdims=True)) + LAM_INIT)
    gain = jnp.concatenate([g_ref[...]] * (tq // LANES), axis=1)
    ys = []
    for h in range(2):
        a0 = acc_s[2 * h]
        a1 = acc_s[2 * h + 1]
        a = a0[0:DIFF_V] / a0[DIFF_V:DIFF_V + 1] - lam * (a1[0:DIFF_V] / a1[DIFF_V:DIFF_V + 1])
        ms = jnp.mean(a * a, axis=0, keepdims=True)
        ys.append(a * lax.rsqrt(ms + RMS_EPS) * gain * (1.0 - LAM_INIT))
    o_ref[0] = jnp.concatenate(ys, axis=0).T.astype(BF16)


def _attn(q, k, vt, kabs, bias, slope_tk, lam_vecs, gain):
    b, s, _ = q.shape
    tq, tk = Q_TILE, K_TILE
    pairs = DIFF_HEADS // 2
    return pl.pallas_call(
        _attn_kernel,
        out_shape=jax.ShapeDtypeStruct((b, s, BRANCH), BF16),
        grid=(b, pairs, s // tq),
        in_specs=[pl.BlockSpec((1, tq, LANES), lambda bi, hp, qi: (bi, qi, hp)),
                  pl.BlockSpec((1, s, LANES), lambda bi, hp, qi: (bi, 0, hp)),
                  pl.BlockSpec((1, s // tk, LANES, tk), lambda bi, hp, qi: (bi, 0, hp, 0)),
                  pl.BlockSpec((1, s // tk, 8, LANES), lambda bi, hp, qi: (bi, 0, 0, hp)),
                  pl.BlockSpec((1, tk, LANES), lambda bi, hp, qi: (hp, 0, 0)),
                  pl.BlockSpec((1, 8, LANES), lambda bi, hp, qi: (hp, 0, 0)),
                  _const_spec((8, LANES), single_buffer=False),
                  _const_spec((DIFF_V, LANES), single_buffer=False)],
        out_specs=pl.BlockSpec((1, tq, LANES), lambda bi, hp, qi: (bi, qi, hp)),
        scratch_shapes=[pltpu.VMEM((2, 2 * LANES, 2 * tq), BF16),
                        pltpu.VMEM((8, tq), F32),
                        pltpu.VMEM((4, V_AUG, tq), F32),
                        pltpu.VMEM((2, tk, 2 * tq), F32),
                        pltpu.VMEM((2, tk, 2 * tq), F32),
                        pltpu.VMEM((8, 2 * tq), F32),
                        pltpu.VMEM((8, 2 * tq), F32)],
        compiler_params=pltpu.CompilerParams(dimension_semantics=("parallel", "parallel", "parallel"),
                                             vmem_limit_bytes=VMEM_LIMIT),
        name="attn",
    )(q, k, vt, kabs, bias, slope_tk, lam_vecs, gain)


def _epi_kernel(x_ref, att_ref, sz_ref, part_ref, gd_ref, lng_ref, lnb_ref, wb1_ref, wout_ref, og_ref, ob_ref,
                o_ref):
    h = _layer_norm(x_ref[0], lng_ref[...], lnb_ref[...])
    o_diff = (att_ref[0].astype(F32) * sz_ref[0].astype(F32)).astype(BF16)
    y_diff = jnp.dot(o_diff, wb1_ref[...], preferred_element_type=F32)
    merged = part_ref[0].astype(F32) + gd_ref[0].astype(F32) * y_diff
    out = jnp.dot(merged.astype(BF16), wout_ref[...], preferred_element_type=F32)
    o_ref[0] = _layer_norm(DEEPNORM_ALPHA * h + out, og_ref[...], ob_ref[...])


def _epi(x, att, sz, part, gd, ln_g, ln_b, wb1, w_out, out_g, out_b):
    b, s, _ = x.shape
    tt = TOKEN_TILE
    tok = lambda width: pl.BlockSpec((1, tt, width), lambda bi, si: (bi, si, 0))
    return pl.pallas_call(
        _epi_kernel,
        out_shape=jax.ShapeDtypeStruct((b, s, D_MODEL), F32),
        grid=(b, s // tt),
        in_specs=[tok(D_MODEL), tok(BRANCH), tok(BRANCH), tok(D_MODEL), tok(D_MODEL),
                  _const_spec((1, D_MODEL)), _const_spec((1, D_MODEL)),
                  _const_spec((BRANCH, D_MODEL)), _const_spec((D_MODEL, D_MODEL)),
                  _const_spec((1, D_MODEL)), _const_spec((1, D_MODEL))],
        out_specs=tok(D_MODEL),
        compiler_params=pltpu.CompilerParams(dimension_semantics=("parallel", "parallel"),
                                             vmem_limit_bytes=VMEM_LIMIT),
        name="epi",
    )(x, att, sz, part, gd, ln_g, ln_b, wb1, w_out, out_g, out_b)


def _alibi_tables():
    pos = np.arange(K_TILE)
    lo = (pos % 256).astype(np.float32)
    hi = (pos - pos % 256).astype(np.float32)
    slopes = [2.0 ** (-8.0 * (h + 1) / DIFF_HEADS) for h in range(DIFF_HEADS)]
    bias = np.zeros((DIFF_HEADS // 2, K_TILE, LANES), np.float32)
    slope_tk = np.zeros((DIFF_HEADS // 2, 8, LANES), np.float32)
    for hp in range(DIFF_HEADS // 2):
        for h in range(2):
            sl = slopes[2 * hp + h]
            for t in range(len(LOG2E_TERMS)):
                bias[hp, :, ALIBI_COLS * h + 2 * t] = sl * lo
                bias[hp, :, ALIBI_COLS * h + 2 * t + 1] = sl * hi
            slope_tk[hp, h, :] = sl * K_TILE * LOG2E
    return jnp.asarray(bias, BF16), jnp.asarray(slope_tk, F32)


def kernel(x, mem, ln_in_g, ln_in_b, w_in, b_gate, pool_w, pool_scale, lambda_q1, lambda_k1, lambda_q2, lambda_k2,
           diff_norm_g, w_mem_kv, w_branch, w_out, ln_out_g, ln_out_b):
    assert w_in.shape[0] == DEPTH == 1
    w = w_in[0]
    seg = lambda i: w[:, i * BRANCH:(i + 1) * BRANCH]
    w_main = jnp.concatenate([seg(0), seg(1), seg(2), seg(3), seg(5), seg(6), seg(7)], axis=1).astype(BF16)
    w_vt = seg(4).T.astype(BF16)
    w_g = w[:, 8 * BRANCH:].astype(BF16)
    row = lambda v: v.reshape(1, -1).astype(F32)

    mk, mv = _memkv(mem, w_mem_kv[0].astype(BF16))
    q, k, vt, sz, part, gd, kabs = _proj(
        x, row(ln_in_g), row(ln_in_b), w_main, w_vt, w_g, row(b_gate[0]), pool_w[0].astype(BF16),
        row(pool_scale[0]), mk, mv, w_branch[0, 0].astype(BF16), w_branch[0, 2].astype(BF16))

    bias, slope_tk = _alibi_tables()
    lam_vecs = jnp.concatenate([lambda_q1, lambda_k1, lambda_q2, lambda_k2], axis=0).astype(F32)
    lam_vecs = jnp.pad(lam_vecs, ((0, 4), (0, LANES - DIFF_QK)))
    gain = jnp.broadcast_to(diff_norm_g[0].astype(F32)[:, None], (DIFF_V, LANES))
    att = _attn(q, k, vt, kabs, bias, slope_tk, lam_vecs, gain)

    return _epi(x, att, sz, part, gd, row(ln_in_g), row(ln_in_b), w_branch[0, 1].astype(BF16),
                w_out[0].astype(BF16), row(ln_out_g[0]), row(ln_out_b[0]))
```

```python
import math
import struct

import jax
import jax.numpy as jnp
import numpy as np
from jax import lax
from jax.experimental import pallas as pl
from jax.experimental.pallas import tpu as pltpu

F32 = jnp.float32
BF16 = jnp.bfloat16

D_MODEL = 1024
BRANCH = 512
MEM_LEN = 256
POOL_WINDOWS = (2, 4, 8, 16)
POOL_GROUP = 128
DIFF_HEADS = 8
DIFF_QK = 32
DIFF_V = 64
MEM_HEADS = 4
MEM_HEAD_DIM = 128
LN_EPS = 1e-5
RMS_EPS = 1e-5
DEPTH = 1
DEEPNORM_ALPHA = (2.0 * DEPTH) ** 0.25
LAM_INIT = 0.8 - 0.6 * math.exp(-0.3 * 0)
LOG2E = math.log2(math.e)
DIFF_SCALE = DIFF_QK ** -0.5 * LOG2E


def _bf16_terms(x, n):
    terms = []
    for _ in range(n):
        bits = struct.unpack("<I", struct.pack("<f", x))[0]
        bits = (bits + 0x7FFF + ((bits >> 16) & 1)) & 0xFFFF0000
        t = struct.unpack("<f", struct.pack("<I", bits))[0]
        terms.append(t)
        x -= t
    return terms


LOG2E_TERMS = _bf16_terms(LOG2E, 3)
MEM_SCALE = MEM_HEAD_DIM ** -0.5
NEG_BIG = -1e30
ZERO_PROB_EXPONENT = 151.0
BOUND_SLACK_REL = 1.001
BOUND_SLACK_ABS = 1.0

TOKEN_TILE = 512
Q_TILE = 1024
K_TILE = TOKEN_TILE
CHUNK = 256
EPI_ROWS = 128
HALO = 32
LANES = 128
V_AUG = DIFF_V + 16
ALIBI_COLS = 2 * len(LOG2E_TERMS)
VMEM_LIMIT = 56 * 1024 * 1024

NT_DIMS = (((1,), (1,)), ((), ()))


def _layer_norm(x, g, b):
    mu = jnp.mean(x, axis=-1, keepdims=True)
    xc = x - mu
    var = jnp.mean(xc * xc, axis=-1, keepdims=True)
    return xc * lax.rsqrt(var + LN_EPS) * g + b


def _silu(x):
    return x * jax.nn.sigmoid(x)


def _const_spec(shape, single_buffer=True):
    n = len(shape)
    mode = pl.Buffered(1) if single_buffer else None
    return pl.BlockSpec(shape, lambda *_: (0,) * n, pipeline_mode=mode)


def _memkv_kernel(mem_ref, w_ref, mk_ref, mv_ref):
    kv = jnp.dot(mem_ref[0].astype(BF16), w_ref[...], preferred_element_type=F32)
    mk_ref[0] = kv[:, :BRANCH].astype(BF16)
    mv_ref[0] = kv[:, BRANCH:].astype(BF16)


def _memkv(mem, w_kv):
    b = mem.shape[0]
    return pl.pallas_call(
        _memkv_kernel,
        out_shape=(jax.ShapeDtypeStruct((b, MEM_LEN, BRANCH), BF16),
                   jax.ShapeDtypeStruct((b, MEM_LEN, BRANCH), BF16)),
        grid=(b,),
        in_specs=[pl.BlockSpec((1, MEM_LEN, D_MODEL), lambda i: (i, 0, 0)),
                  _const_spec((D_MODEL, 2 * BRANCH))],
        out_specs=(pl.BlockSpec((1, MEM_LEN, BRANCH), lambda i: (i, 0, 0)),
                   pl.BlockSpec((1, MEM_LEN, BRANCH), lambda i: (i, 0, 0))),
        name="memkv",
    )(mem, w_kv)


def _proj_kernel(x_ref, lng_ref, lnb_ref, wmain_ref, wvt_ref, wg_ref, bg_ref, poolw_ref, pscale_ref,
                 mk_ref, mv_ref, wb0_ref, wb2_ref,
                 q_ref, k_ref, vt_ref, sz_ref, part_ref, kabs_ref,
                 e1_ref, e2_ref, e4_ref, e8_ref):
    tt = TOKEN_TILE
    s_idx = pl.program_id(1)
    h = _layer_norm(x_ref[0], lng_ref[...], lnb_ref[...])
    hb = h.astype(BF16)

    def proj(seg):
        return jnp.dot(hb, wmain_ref[:, seg * BRANCH:(seg + 1) * BRANCH], preferred_element_type=F32)

    q_ref[0] = (proj(2) * DIFF_SCALE).astype(BF16)
    kb = proj(3).astype(BF16)
    k_ref[0] = kb
    kabs_ref[0, 0] = jnp.broadcast_to(jnp.max(jnp.abs(kb.astype(F32)), axis=0, keepdims=True), (8, BRANCH))
    vt = lax.dot_general(wvt_ref[...], hb, NT_DIMS, preferred_element_type=F32)
    vt_ref[0, 0] = vt.astype(BF16)
    sz_ref[0] = _silu(proj(4)).astype(BF16)

    u = proj(0)

    @pl.when(s_idx == 0)
    def _():
        e1_ref[0:HALO, :] = jnp.zeros((HALO, BRANCH), F32)

    e1_ref[HALO:HALO + tt, :] = u
    n = HALO + tt
    e2_ref[8:n, :] = e1_ref[8:n, :] + e1_ref[7:n - 1, :]
    e4_ref[16:n, :] = e2_ref[16:n, POOL_GROUP:] + e2_ref[14:n - 2, POOL_GROUP:]
    e8_ref[24:n, :] = e4_ref[24:n, POOL_GROUP:] + e4_ref[20:n - 4, POOL_GROUP:]
    s16 = e8_ref[HALO:n, POOL_GROUP:] + e8_ref[HALO - 8:n - 8, POOL_GROUP:]
    wsum = (e2_ref[HALO:n, 0:POOL_GROUP], e4_ref[HALO:n, 0:POOL_GROUP], e8_ref[HALO:n, 0:POOL_GROUP], s16)
    e1_ref[0:HALO, :] = e1_ref[tt:tt + HALO, :]

    t_pos = s_idx * tt + lax.broadcasted_iota(jnp.int32, (tt, POOL_GROUP), 0)
    pooled = []
    for g, w in enumerate(POOL_WINDOWS):
        cnt = jnp.minimum(t_pos + 1, w).astype(F32)
        pg = wsum[g] / cnt - u[:, g * POOL_GROUP:(g + 1) * POOL_GROUP]
        pooled.append(jnp.dot(pg.astype(BF16), poolw_ref[g], preferred_element_type=F32))
    o_pool = jnp.concatenate(pooled, axis=1) * pscale_ref[...] * _silu(proj(1))

    mq = (proj(5) * MEM_SCALE).astype(BF16)
    heads = []
    for hh in range(MEM_HEADS):
        sl = slice(hh * MEM_HEAD_DIM, (hh + 1) * MEM_HEAD_DIM)
        s = lax.dot_general(mq[:, sl], mk_ref[0, :, sl], NT_DIMS, preferred_element_type=F32)
        p = jnp.exp(s - jnp.max(s, axis=-1, keepdims=True))
        l = jnp.sum(p, axis=-1, keepdims=True)
        heads.append(jnp.dot(p.astype(BF16), mv_ref[0, :, sl], preferred_element_type=F32) / l)
    o_mem = jnp.concatenate(heads, axis=1) * _silu(proj(6))

    def gate(nb):
        sl = slice(nb * D_MODEL, (nb + 1) * D_MODEL)
        return jax.nn.sigmoid(jnp.dot(hb, wg_ref[:, sl], preferred_element_type=F32) + bg_ref[:, sl])

    y_pool = jnp.dot(o_pool.astype(BF16), wb0_ref[...], preferred_element_type=F32)
    y_mem = jnp.dot(o_mem.astype(BF16), wb2_ref[...], preferred_element_type=F32)
    part_ref[0] = (gate(0) * y_pool + gate(1) * y_mem).astype(BF16)


def _proj(x, ln_g, ln_b, w_main, w_vt, w_g, b_g, pool_w, pool_scale, mk, mv, wb0, wb2):
    b, s, _ = x.shape
    tt = TOKEN_TILE
    ns = s // tt
    tok = lambda width: pl.BlockSpec((1, tt, width), lambda bi, si: (bi, si, 0))
    per_batch = pl.BlockSpec((1, MEM_LEN, BRANCH), lambda bi, si: (bi, 0, 0))
    return pl.pallas_call(
        _proj_kernel,
        out_shape=(jax.ShapeDtypeStruct((b, s, BRANCH), BF16),
                   jax.ShapeDtypeStruct((b, s, BRANCH), BF16),
                   jax.ShapeDtypeStruct((b, ns, BRANCH, tt), BF16),
                   jax.ShapeDtypeStruct((b, s, BRANCH), BF16),
                   jax.ShapeDtypeStruct((b, s, D_MODEL), BF16),
                   jax.ShapeDtypeStruct((b, ns, 8, BRANCH), F32)),
        grid=(b, ns),
        in_specs=[tok(D_MODEL),
                  _const_spec((1, D_MODEL)), _const_spec((1, D_MODEL)),
                  _const_spec((D_MODEL, 7 * BRANCH)), _const_spec((BRANCH, D_MODEL)),
                  _const_spec((D_MODEL, 2 * D_MODEL)), _const_spec((1, 2 * D_MODEL)),
                  _const_spec((len(POOL_WINDOWS), POOL_GROUP, POOL_GROUP)), _const_spec((1, BRANCH)),
                  per_batch, per_batch,
                  _const_spec((BRANCH, D_MODEL)), _const_spec((BRANCH, D_MODEL))],
        out_specs=(tok(BRANCH), tok(BRANCH),
                   pl.BlockSpec((1, 1, BRANCH, tt), lambda bi, si: (bi, si, 0, 0)),
                   tok(BRANCH), tok(D_MODEL),
                   pl.BlockSpec((1, 1, 8, BRANCH), lambda bi, si: (bi, si, 0, 0))),
        scratch_shapes=[pltpu.VMEM((HALO + tt, BRANCH), F32),
                        pltpu.VMEM((HALO + tt, BRANCH), F32),
                        pltpu.VMEM((HALO + tt, BRANCH - POOL_GROUP), F32),
                        pltpu.VMEM((HALO + tt, BRANCH - 2 * POOL_GROUP), F32)],
        compiler_params=pltpu.CompilerParams(dimension_semantics=("arbitrary", "arbitrary"),
                                             vmem_limit_bytes=VMEM_LIMIT),
        name="proj",
    )(x, ln_g, ln_b, w_main, w_vt, w_g, b_g, pool_w, pool_scale, mk, mv, wb0, wb2)


def _attn_kernel(q_ref, k_ref, vt_ref, kabs_ref, sz_ref, bias_ref, slope_ref, lam_ref, g_ref, o_ref,
                 qt_s, m_s, acc_s,
                 sa_s, sb_s, tma_s, tmb_s):
    tq, tk = Q_TILE, K_TILE
    i = pl.program_id(2)

    qt32 = q_ref[0].astype(F32).T
    qt = qt32.astype(BF16)
    kcol = jnp.max(kabs_ref[0], axis=0).T[:, 0:1]
    qk_bound = jnp.abs(qt32) * kcol
    score_bound = [jnp.sum(qk_bound[DIFF_QK * idx:DIFF_QK * (idx + 1)], axis=0, keepdims=True)
                   for idx in range(4)]
    @pl.when(i == 0)
    def _():
        row = lax.broadcasted_iota(jnp.int32, (16, 2 * tq), 0)
        for h in range(2):
            qt_s[h] = jnp.zeros((2 * LANES, 2 * tq), BF16)
            alibi_rows = jnp.zeros((16, 2 * tq), F32)
            for t, c in enumerate(LOG2E_TERMS):
                r0 = ALIBI_COLS * h + 2 * t
                alibi_rows = jnp.where((row == r0) | (row == r0 + 1), c, alibi_rows)
            qt_s[h, LANES:LANES + 16, :] = alibi_rows.astype(BF16)

    for h in range(2):
        base = 2 * DIFF_QK * h
        qt_s[h, base:base + DIFF_QK, 0:tq] = qt[base:base + DIFF_QK]
        qt_s[h, base + DIFF_QK:base + 2 * DIFF_QK, tq:2 * tq] = qt[base + DIFF_QK:base + 2 * DIFF_QK]
    m_s[...] = jnp.full(m_s.shape, NEG_BIG, F32)
    acc_s[...] = jnp.zeros(acc_s.shape, F32)

    ones_rows = jnp.where(lax.broadcasted_iota(jnp.int32, (16, tk), 0) == 0, 1.0, 0.0).astype(BF16)
    slope_tk = [jnp.concatenate([slope_ref[0, h:h + 1, :]] * (tq // LANES), axis=1) for h in range(2)]

    cw = CHUNK
    chunks = [(h, n) for h in range(2) for n in range(2 * tq // cw)]

    def q_start(n):
        return (n * cw) % tq

    def score_chunk(lhs, s_buf, tm_buf, h, n, key_start):
        cols = slice(n * cw, (n + 1) * cw)
        st = jnp.dot(lhs, qt_s[h, :, cols], preferred_element_type=F32)
        if key_start is not None and key_start + tk - 1 > q_start(n):
            kk = lax.broadcasted_iota(jnp.int32, (tk, cw), 0) + key_start
            qq = lax.broadcasted_iota(jnp.int32, (tk, cw), 1) + q_start(n)
            st = jnp.where(kk <= qq, st, NEG_BIG)
        s_buf[h, :, cols] = st
        tm_buf[h:h + 1, cols] = jnp.max(st, axis=0, keepdims=True)

    def value_chunk(vblk, dj, s_buf, tm_buf, h, n):
        idx = 2 * h + (n * cw) // tq
        cols = slice(n * cw, (n + 1) * cw)
        qcols = slice(q_start(n), q_start(n) + cw)
        off = slope_tk[h][:, qcols] * dj
        vaug = jnp.concatenate([vblk[DIFF_V * h:DIFF_V * (h + 1)], ones_rows], axis=0)
        m_old = m_s[idx:idx + 1, qcols]
        m_new = jnp.maximum(m_old, tm_buf[h:h + 1, cols] + off)
        alpha = jnp.exp2(m_old - m_new)
        p = jnp.exp2(s_buf[h, :, cols] - (m_new - off)).astype(BF16)
        pv = jnp.dot(vaug, p, preferred_element_type=F32)
        acc_s[idx, :, qcols] = alpha * acc_s[idx, :, qcols] + pv
        m_s[idx:idx + 1, qcols] = m_new

    def key_lhs(blk):
        kblk = k_ref[0, pl.ds(pl.multiple_of(blk * tk, tk), tk), :]
        return jnp.concatenate([kblk, bias_ref[0]], axis=1)

    def stage(score=None, value=None):
        s_chunks, v_chunks = [], []
        if score is not None:
            s_blk, s_bufs, key_start, s_chunks = score
            lhs = key_lhs(s_blk)
        if value is not None:
            v_blk, v_bufs, v_chunks = value
            vblk = vt_ref[0, v_blk]
            dj = jnp.full((1, cw), v_blk - first_diag, jnp.int32).astype(F32)
        for c in range(max(len(s_chunks), len(v_chunks))):
            if c < len(s_chunks):
                score_chunk(lhs, *s_bufs, *s_chunks[c], key_start)
            if c < len(v_chunks):
                value_chunk(vblk, dj, *v_bufs, *v_chunks[c])

    buf_a, buf_b = (sa_s, tma_s), (sb_s, tmb_s)
    first_diag = (tq // tk) * i
    head = [[c for c in chunks if c[0] == h] for h in range(2)]
    late = [[c for c in head[h] if q_start(c[1]) >= tk] for h in range(2)]
    stage(score=(first_diag + 1, buf_a, tk, late[0]))
    stage(score=(first_diag + 1, buf_a, tk, late[1]), value=(first_diag + 1, buf_a, late[0]))
    stage(score=(first_diag, buf_b, 0, head[0]), value=(first_diag + 1, buf_a, late[1]))
    stage(score=(first_diag, buf_b, 0, head[1]), value=(first_diag, buf_b, head[0]))

    def pair(p, prev):
        stage(score=(2 * p, buf_a, None, head[0]), value=(prev, buf_b, head[1]))
        stage(score=(2 * p, buf_a, None, head[1]), value=(2 * p, buf_a, head[0]))
        stage(score=(2 * p + 1, buf_b, None, head[0]), value=(2 * p, buf_a, head[1]))
        stage(score=(2 * p + 1, buf_b, None, head[1]), value=(2 * p + 1, buf_b, head[0]))
        return 2 * p + 1

    tiles_needed = None
    for h in range(2):
        gap = None
        for mm in range(2):
            idx = 2 * h + mm
            m_now = jnp.maximum(m_s[idx:idx + 1, :], tmb_s[h:h + 1, mm * tq:(mm + 1) * tq])
            g = jnp.max(score_bound[idx] * BOUND_SLACK_REL + BOUND_SLACK_ABS - m_now, axis=1, keepdims=True)
            gap = g if gap is None else jnp.maximum(gap, g)
        need_h = (gap + ZERO_PROB_EXPONENT) / slope_ref[0, h:h + 1, 0:1] + (tk - 1) / tk
        tiles_needed = need_h if tiles_needed is None else jnp.maximum(tiles_needed, need_h)
    n_full = jnp.full((1, 1), first_diag, jnp.int32)
    first_tile = jnp.ceil(n_full.astype(F32) - tiles_needed - 0.01).astype(jnp.int32)
    first_pair = jnp.max(jnp.clip(first_tile, 0, n_full)) // 2

    last = lax.fori_loop(first_pair, i, pair, first_diag)
    stage(value=(last, buf_b, head[1]))

    lamv = lam_ref[...]
    lam = (jnp.exp(jnp.sum(lamv[0:1] * lamv[1:2], axis=1, keepdims=True))
           - jnp.exp(jnp.sum(lamv[2:3] * lamv[3:4], axis=1, keepdims=True)) + LAM_INIT)
    gain = jnp.concatenate([g_ref[...]] * (tq // LANES), axis=1)
    ys = []
    for h in range(2):
        a0 = acc_s[2 * h]
        a1 = acc_s[2 * h + 1]
        a = a0[0:DIFF_V] / a0[DIFF_V:DIFF_V + 1] - lam * (a1[0:DIFF_V] / a1[DIFF_V:DIFF_V + 1])
        ms = jnp.mean(a * a, axis=0, keepdims=True)
        ys.append(a * lax.rsqrt(ms + RMS_EPS) * gain * (1.0 - LAM_INIT))
    o_ref[0] = (jnp.concatenate(ys, axis=0).T * sz_ref[0].astype(F32)).astype(BF16)


def _attn(q, k, vt, kabs, sz, bias, slope_tk, lam_vecs, gain):
    b, s, _ = q.shape
    tq, tk = Q_TILE, K_TILE
    pairs = DIFF_HEADS // 2
    return pl.pallas_call(
        _attn_kernel,
        out_shape=jax.ShapeDtypeStruct((b, s, BRANCH), BF16),
        grid=(b, pairs, s // tq),
        in_specs=[pl.BlockSpec((1, tq, LANES), lambda bi, hp, qi: (bi, qi, hp)),
                  pl.BlockSpec((1, s, LANES), lambda bi, hp, qi: (bi, 0, hp)),
                  pl.BlockSpec((1, s // tk, LANES, tk), lambda bi, hp, qi: (bi, 0, hp, 0)),
                  pl.BlockSpec((1, s // tk, 8, LANES), lambda bi, hp, qi: (bi, 0, 0, hp)),
                  pl.BlockSpec((1, tq, LANES), lambda bi, hp, qi: (bi, qi, hp)),
                  pl.BlockSpec((1, tk, LANES), lambda bi, hp, qi: (hp, 0, 0)),
                  pl.BlockSpec((1, 8, LANES), lambda bi, hp, qi: (hp, 0, 0)),
                  _const_spec((8, LANES), single_buffer=False),
                  _const_spec((DIFF_V, LANES), single_buffer=False)],
        out_specs=pl.BlockSpec((1, tq, LANES), lambda bi, hp, qi: (bi, qi, hp)),
        scratch_shapes=[pltpu.VMEM((2, 2 * LANES, 2 * tq), BF16),
                        pltpu.VMEM((8, tq), F32),
                        pltpu.VMEM((4, V_AUG, tq), F32),
                        pltpu.VMEM((2, tk, 2 * tq), F32),
                        pltpu.VMEM((2, tk, 2 * tq), F32),
                        pltpu.VMEM((8, 2 * tq), F32),
                        pltpu.VMEM((8, 2 * tq), F32)],
        compiler_params=pltpu.CompilerParams(dimension_semantics=("parallel", "parallel", "arbitrary"),
                                             vmem_limit_bytes=VMEM_LIMIT),
        name="attn",
    )(q, k, vt, kabs, sz, bias, slope_tk, lam_vecs, gain)


def _epi_kernel(x_ref, odiff_ref, part_ref, lng_ref, lnb_ref, wg_ref, bg_ref, wb1_ref, wout_ref, og_ref, ob_ref,
                o_ref):
    for r in range(0, TOKEN_TILE, EPI_ROWS):
        rows = slice(r, r + EPI_ROWS)
        h = _layer_norm(x_ref[0, rows], lng_ref[...], lnb_ref[...])
        g_diff = jax.nn.sigmoid(jnp.dot(h.astype(BF16), wg_ref[...], preferred_element_type=F32) + bg_ref[...])
        y_diff = jnp.dot(odiff_ref[0, rows], wb1_ref[...], preferred_element_type=F32)
        merged = part_ref[0, rows].astype(F32) + g_diff * y_diff
        out = jnp.dot(merged.astype(BF16), wout_ref[...], preferred_element_type=F32)
        o_ref[0, rows] = _layer_norm(DEEPNORM_ALPHA * h + out, og_ref[...], ob_ref[...])


def _epi(x, o_diff, part, ln_g, ln_b, w_gd, b_gd, wb1, w_out, out_g, out_b):
    b, s, _ = x.shape
    tt = TOKEN_TILE
    tok = lambda width: pl.BlockSpec((1, tt, width), lambda bi, si: (bi, si, 0))
    return pl.pallas_call(
        _epi_kernel,
        out_shape=jax.ShapeDtypeStruct((b, s, D_MODEL), F32),
        grid=(b, s // tt),
        in_specs=[tok(D_MODEL), tok(BRANCH), tok(D_MODEL),
                  _const_spec((1, D_MODEL)), _const_spec((1, D_MODEL)),
                  _const_spec((D_MODEL, D_MODEL)), _const_spec((1, D_MODEL)),
                  _const_spec((BRANCH, D_MODEL)), _const_spec((D_MODEL, D_MODEL)),
                  _const_spec((1, D_MODEL)), _const_spec((1, D_MODEL))],
        out_specs=tok(D_MODEL),
        compiler_params=pltpu.CompilerParams(dimension_semantics=("parallel", "parallel"),
                                             vmem_limit_bytes=VMEM_LIMIT),
        name="epi",
    )(x, o_diff, part, ln_g, ln_b, w_gd, b_gd, wb1, w_out, out_g, out_b)


def _alibi_tables():
    pos = np.arange(K_TILE)
    lo = (pos % 256).astype(np.float32)
    hi = (pos - pos % 256).astype(np.float32)
    slopes = [2.0 ** (-8.0 * (h + 1) / DIFF_HEADS) for h in range(DIFF_HEADS)]
    bias = np.zeros((DIFF_HEADS // 2, K_TILE, LANES), np.float32)
    slope_tk = np.zeros((DIFF_HEADS // 2, 8, LANES), np.float32)
    for hp in range(DIFF_HEADS // 2):
        for h in range(2):
            sl = slopes[2 * hp + h]
            for t in range(len(LOG2E_TERMS)):
                bias[hp, :, ALIBI_COLS * h + 2 * t] = sl * lo
                bias[hp, :, ALIBI_COLS * h + 2 * t + 1] = sl * hi
            slope_tk[hp, h, :] = sl * K_TILE * LOG2E
    return jnp.asarray(bias, BF16), jnp.asarray(slope_tk, F32)


def kernel(x, mem, ln_in_g, ln_in_b, w_in, b_gate, pool_w, pool_scale, lambda_q1, lambda_k1, lambda_q2, lambda_k2,
           diff_norm_g, w_mem_kv, w_branch, w_out, ln_out_g, ln_out_b):
    assert w_in.shape[0] == DEPTH == 1
    w = w_in[0]
    seg = lambda i: w[:, i * BRANCH:(i + 1) * BRANCH]
    w_main = jnp.concatenate([seg(0), seg(1), seg(2), seg(3), seg(5), seg(6), seg(7)], axis=1).astype(BF16)
    w_vt = seg(4).T.astype(BF16)
    gate_w = lambda n: w[:, 8 * BRANCH + n * D_MODEL:8 * BRANCH + (n + 1) * D_MODEL]
    gate_b = lambda n: b_gate[0, n * D_MODEL:(n + 1) * D_MODEL]
    w_g = jnp.concatenate([gate_w(0), gate_w(2)], axis=1).astype(BF16)
    b_g = jnp.concatenate([gate_b(0), gate_b(2)])
    row = lambda v: v.reshape(1, -1).astype(F32)

    mk, mv = _memkv(mem, w_mem_kv[0].astype(BF16))
    q, k, vt, sz, part, kabs = _proj(
        x, row(ln_in_g), row(ln_in_b), w_main, w_vt, w_g, row(b_g), pool_w[0].astype(BF16),
        row(pool_scale[0]), mk, mv, w_branch[0, 0].astype(BF16), w_branch[0, 2].astype(BF16))

    bias, slope_tk = _alibi_tables()
    lam_vecs = jnp.concatenate([lambda_q1, lambda_k1, lambda_q2, lambda_k2], axis=0).astype(F32)
    lam_vecs = jnp.pad(lam_vecs, ((0, 4), (0, LANES - DIFF_QK)))
    gain = jnp.broadcast_to(diff_norm_g[0].astype(F32)[:, None], (DIFF_V, LANES))
    o_diff = _attn(q, k, vt, kabs, sz, bias, slope_tk, lam_vecs, gain)

    return _epi(x, o_diff, part, row(ln_in_g), row(ln_in_b), gate_w(1).astype(BF16), row(gate_b(1)),
                w_branch[0, 1].astype(BF16), w_out[0].astype(BF16), row(ln_out_g[0]), row(ln_out_b[0]))
```

```python
import math
import struct

import jax
import jax.numpy as jnp
import numpy as np
from jax import lax
from jax.experimental import pallas as pl
from jax.experimental.pallas import tpu as pltpu

F32 = jnp.float32
BF16 = jnp.bfloat16

D_MODEL = 1024
BRANCH = 512
MEM_LEN = 256
POOL_WINDOWS = (2, 4, 8, 16)
POOL_GROUP = 128
DIFF_HEADS = 8
DIFF_QK = 32
DIFF_V = 64
MEM_HEADS = 4
MEM_HEAD_DIM = 128
LN_EPS = 1e-5
RMS_EPS = 1e-5
DEPTH = 1
DEEPNORM_ALPHA = (2.0 * DEPTH) ** 0.25
LAM_INIT = 0.8 - 0.6 * math.exp(-0.3 * 0)
LOG2E = math.log2(math.e)
DIFF_SCALE = DIFF_QK ** -0.5 * LOG2E


def _bf16_terms(x, n):
    terms = []
    for _ in range(n):
        bits = struct.unpack("<I", struct.pack("<f", x))[0]
        bits = (bits + 0x7FFF + ((bits >> 16) & 1)) & 0xFFFF0000
        t = struct.unpack("<f", struct.pack("<I", bits))[0]
        terms.append(t)
        x -= t
    return terms


LOG2E_TERMS = _bf16_terms(LOG2E, 3)
MEM_SCALE = MEM_HEAD_DIM ** -0.5
NEG_BIG = -1e30
ZERO_PROB_EXPONENT = 151.0
BOUND_SLACK_REL = 1.001
BOUND_SLACK_ABS = 1.0

TOKEN_TILE = 512
Q_TILE = 1024
K_TILE = TOKEN_TILE
CHUNK = 256
HALO = 32
LANES = 128
V_AUG = DIFF_V + 16
ALIBI_COLS = 2 * len(LOG2E_TERMS)


def _alibi_lane(h):
    return 2 * DIFF_QK * (1 - h)
VMEM_LIMIT = 56 * 1024 * 1024

NT_DIMS = (((1,), (1,)), ((), ()))


def _layer_norm(x, g, b):
    mu = jnp.mean(x, axis=-1, keepdims=True)
    xc = x - mu
    var = jnp.mean(xc * xc, axis=-1, keepdims=True)
    return xc * lax.rsqrt(var + LN_EPS) * g + b


def _silu(x):
    return x * jax.nn.sigmoid(x)


def _const_spec(shape, single_buffer=True):
    n = len(shape)
    mode = pl.Buffered(1) if single_buffer else None
    return pl.BlockSpec(shape, lambda *_: (0,) * n, pipeline_mode=mode)


def _memkv_kernel(mem_ref, w_ref, mk_ref, mv_ref):
    kv = jnp.dot(mem_ref[0].astype(BF16), w_ref[...], preferred_element_type=F32)
    mk_ref[0] = kv[:, :BRANCH].astype(BF16)
    mv_ref[0] = kv[:, BRANCH:].astype(BF16)


def _memkv(mem, w_kv):
    b = mem.shape[0]
    return pl.pallas_call(
        _memkv_kernel,
        out_shape=(jax.ShapeDtypeStruct((b, MEM_LEN, BRANCH), BF16),
                   jax.ShapeDtypeStruct((b, MEM_LEN, BRANCH), BF16)),
        grid=(b,),
        in_specs=[pl.BlockSpec((1, MEM_LEN, D_MODEL), lambda i: (i, 0, 0)),
                  _const_spec((D_MODEL, 2 * BRANCH))],
        out_specs=(pl.BlockSpec((1, MEM_LEN, BRANCH), lambda i: (i, 0, 0)),
                   pl.BlockSpec((1, MEM_LEN, BRANCH), lambda i: (i, 0, 0))),
        name="memkv",
    )(mem, w_kv)


def _proj_kernel(x_ref, lng_ref, lnb_ref, wmain_ref, wvt_ref, wg_ref, bg_ref, poolw_ref, pscale_ref,
                 mk_ref, mv_ref, wb0_ref, wb2_ref,
                 q_ref, k_ref, vt_ref, sz_ref, part_ref, gd_ref, kabs_ref,
                 e1_ref, e2_ref, e4_ref, e8_ref):
    tt = TOKEN_TILE
    s_idx = pl.program_id(1)
    h = _layer_norm(x_ref[0], lng_ref[...], lnb_ref[...])
    hb = h.astype(BF16)

    def proj(seg):
        return jnp.dot(hb, wmain_ref[:, seg * BRANCH:(seg + 1) * BRANCH], preferred_element_type=F32)

    q_ref[0] = (proj(2) * DIFF_SCALE).astype(BF16)
    kb = proj(3).astype(BF16)
    k_ref[0] = kb
    kabs_ref[0, 0] = jnp.broadcast_to(jnp.max(jnp.abs(kb.astype(F32)), axis=0, keepdims=True), (8, BRANCH))
    vt = lax.dot_general(wvt_ref[...], hb, NT_DIMS, preferred_element_type=F32)
    vt_ref[0, 0] = vt.astype(BF16)
    sz_ref[0] = _silu(proj(4)).astype(BF16)

    u = proj(0)

    @pl.when(s_idx == 0)
    def _():
        e1_ref[0:HALO, :] = jnp.zeros((HALO, BRANCH), F32)

    e1_ref[HALO:HALO + tt, :] = u
    n = HALO + tt
    e2_ref[8:n, :] = e1_ref[8:n, :] + e1_ref[7:n - 1, :]
    e4_ref[16:n, :] = e2_ref[16:n, POOL_GROUP:] + e2_ref[14:n - 2, POOL_GROUP:]
    e8_ref[24:n, :] = e4_ref[24:n, POOL_GROUP:] + e4_ref[20:n - 4, POOL_GROUP:]
    s16 = e8_ref[HALO:n, POOL_GROUP:] + e8_ref[HALO - 8:n - 8, POOL_GROUP:]
    wsum = (e2_ref[HALO:n, 0:POOL_GROUP], e4_ref[HALO:n, 0:POOL_GROUP], e8_ref[HALO:n, 0:POOL_GROUP], s16)
    e1_ref[0:HALO, :] = e1_ref[tt:tt + HALO, :]

    t_pos = s_idx * tt + lax.broadcasted_iota(jnp.int32, (tt, POOL_GROUP), 0)
    pooled = []
    for g, w in enumerate(POOL_WINDOWS):
        cnt = jnp.minimum(t_pos + 1, w).astype(F32)
        pg = wsum[g] / cnt - u[:, g * POOL_GROUP:(g + 1) * POOL_GROUP]
        pooled.append(jnp.dot(pg.astype(BF16), poolw_ref[g], preferred_element_type=F32))
    o_pool = jnp.concatenate(pooled, axis=1) * pscale_ref[...] * _silu(proj(1))

    mq = (proj(5) * MEM_SCALE).astype(BF16)
    heads = []
    for hh in range(MEM_HEADS):
        sl = slice(hh * MEM_HEAD_DIM, (hh + 1) * MEM_HEAD_DIM)
        s = lax.dot_general(mq[:, sl], mk_ref[0, :, sl], NT_DIMS, preferred_element_type=F32)
        p = jnp.exp(s - jnp.max(s, axis=-1, keepdims=True))
        l = jnp.sum(p, axis=-1, keepdims=True)
        heads.append(jnp.dot(p.astype(BF16), mv_ref[0, :, sl], preferred_element_type=F32) / l)
    o_mem = jnp.concatenate(heads, axis=1) * _silu(proj(6))

    def gate(nb):
        sl = slice(nb * D_MODEL, (nb + 1) * D_MODEL)
        return jax.nn.sigmoid(jnp.dot(hb, wg_ref[:, sl], preferred_element_type=F32) + bg_ref[:, sl])

    y_pool = jnp.dot(o_pool.astype(BF16), wb0_ref[...], preferred_element_type=F32)
    y_mem = jnp.dot(o_mem.astype(BF16), wb2_ref[...], preferred_element_type=F32)
    part_ref[0] = (gate(0) * y_pool + gate(2) * y_mem).astype(BF16)
    gd_ref[0] = gate(1).astype(BF16)


def _proj(x, ln_g, ln_b, w_main, w_vt, w_g, b_g, pool_w, pool_scale, mk, mv, wb0, wb2):
    b, s, _ = x.shape
    tt = TOKEN_TILE
    ns = s // tt
    tok = lambda width: pl.BlockSpec((1, tt, width), lambda bi, si: (bi, si, 0))
    per_batch = pl.BlockSpec((1, MEM_LEN, BRANCH), lambda bi, si: (bi, 0, 0))
    return pl.pallas_call(
        _proj_kernel,
        out_shape=(jax.ShapeDtypeStruct((b, s, BRANCH), BF16),
                   jax.ShapeDtypeStruct((b, s, BRANCH), BF16),
                   jax.ShapeDtypeStruct((b, ns, BRANCH, tt), BF16),
                   jax.ShapeDtypeStruct((b, s, BRANCH), BF16),
                   jax.ShapeDtypeStruct((b, s, D_MODEL), BF16),
                   jax.ShapeDtypeStruct((b, s, D_MODEL), BF16),
                   jax.ShapeDtypeStruct((b, ns, 8, BRANCH), F32)),
        grid=(b, ns),
        in_specs=[tok(D_MODEL),
                  _const_spec((1, D_MODEL)), _const_spec((1, D_MODEL)),
                  _const_spec((D_MODEL, 7 * BRANCH)), _const_spec((BRANCH, D_MODEL)),
                  _const_spec((D_MODEL, 3 * D_MODEL)), _const_spec((1, 3 * D_MODEL)),
                  _const_spec((len(POOL_WINDOWS), POOL_GROUP, POOL_GROUP)), _const_spec((1, BRANCH)),
                  per_batch, per_batch,
                  _const_spec((BRANCH, D_MODEL)), _const_spec((BRANCH, D_MODEL))],
        out_specs=(tok(BRANCH), tok(BRANCH),
                   pl.BlockSpec((1, 1, BRANCH, tt), lambda bi, si: (bi, si, 0, 0)),
                   tok(BRANCH), tok(D_MODEL), tok(D_MODEL),
                   pl.BlockSpec((1, 1, 8, BRANCH), lambda bi, si: (bi, si, 0, 0))),
        scratch_shapes=[pltpu.VMEM((HALO + tt, BRANCH), F32),
                        pltpu.VMEM((HALO + tt, BRANCH), F32),
                        pltpu.VMEM((HALO + tt, BRANCH - POOL_GROUP), F32),
                        pltpu.VMEM((HALO + tt, BRANCH - 2 * POOL_GROUP), F32)],
        compiler_params=pltpu.CompilerParams(dimension_semantics=("arbitrary", "arbitrary"),
                                             vmem_limit_bytes=VMEM_LIMIT),
        name="proj",
    )(x, ln_g, ln_b, w_main, w_vt, w_g, b_g, pool_w, pool_scale, mk, mv, wb0, wb2)


def _attn_kernel(q_ref, k_ref, vt_ref, kabs_ref, sz_ref, bias_ref, slope_ref, lam_ref, g_ref, o_ref,
                 qt_s, m_s, acc_s,
                 sa_s, sb_s, tma_s, tmb_s):
    tq, tk = Q_TILE, K_TILE
    i = pl.program_id(2)

    qt32 = q_ref[0].astype(F32).T
    qt = qt32.astype(BF16)
    kcol = jnp.max(kabs_ref[0], axis=0).T[:, 0:1]
    qk_bound = jnp.abs(qt32) * kcol
    score_bound = [jnp.sum(qk_bound[DIFF_QK * idx:DIFF_QK * (idx + 1)], axis=0, keepdims=True)
                   for idx in range(4)]
    @pl.when(i == 0)
    def _():
        row = lax.broadcasted_iota(jnp.int32, (16, 2 * tq), 0)
        for h in range(2):
            qt_s[h] = jnp.zeros((LANES, 2 * tq), BF16)
            alibi_rows = jnp.zeros((16, 2 * tq), F32)
            for t, c in enumerate(LOG2E_TERMS):
                alibi_rows = jnp.where((row == 2 * t) | (row == 2 * t + 1), c, alibi_rows)
            qt_s[h, _alibi_lane(h):_alibi_lane(h) + 16, :] = alibi_rows.astype(BF16)

    for h in range(2):
        base = 2 * DIFF_QK * h
        qt_s[h, base:base + DIFF_QK, 0:tq] = qt[base:base + DIFF_QK]
        qt_s[h, base + DIFF_QK:base + 2 * DIFF_QK, tq:2 * tq] = qt[base + DIFF_QK:base + 2 * DIFF_QK]
    m_s[...] = jnp.full(m_s.shape, NEG_BIG, F32)
    acc_s[...] = jnp.zeros(acc_s.shape, F32)

    ones_rows = jnp.where(lax.broadcasted_iota(jnp.int32, (16, tk), 0) == 0, 1.0, 0.0).astype(BF16)
    slope_tk = [jnp.concatenate([slope_ref[0, h:h + 1, :]] * (tq // LANES), axis=1) for h in range(2)]

    cw = CHUNK
    chunks = [(h, n) for h in range(2) for n in range(2 * tq // cw)]

    def q_start(n):
        return (n * cw) % tq

    def score_chunk(lhs, s_buf, tm_buf, h, n, key_start):
        cols = slice(n * cw, (n + 1) * cw)
        st = jnp.dot(lhs[h], qt_s[h, :, cols], preferred_element_type=F32)
        if key_start is not None and key_start + tk - 1 > q_start(n):
            kk = lax.broadcasted_iota(jnp.int32, (tk, cw), 0) + key_start
            qq = lax.broadcasted_iota(jnp.int32, (tk, cw), 1) + q_start(n)
            st = jnp.where(kk <= qq, st, NEG_BIG)
        s_buf[h, :, cols] = st
        tm_buf[h:h + 1, cols] = jnp.max(st, axis=0, keepdims=True)

    def value_chunk(vblk, dj, s_buf, tm_buf, h, n):
        idx = 2 * h + (n * cw) // tq
        cols = slice(n * cw, (n + 1) * cw)
        qcols = slice(q_start(n), q_start(n) + cw)
        off = slope_tk[h][:, qcols] * dj
        vaug = jnp.concatenate([vblk[DIFF_V * h:DIFF_V * (h + 1)], ones_rows], axis=0)
        m_old = m_s[idx:idx + 1, qcols]
        m_new = jnp.maximum(m_old, tm_buf[h:h + 1, cols] + off)
        alpha = jnp.exp2(m_old - m_new)
        p = jnp.exp2(s_buf[h, :, cols] - (m_new - off)).astype(BF16)
        pv = jnp.dot(vaug, p, preferred_element_type=F32)
        acc_s[idx, :, qcols] = alpha * acc_s[idx, :, qcols] + pv
        m_s[idx:idx + 1, qcols] = m_new

    def key_lhs(blk):
        kblk = k_ref[0, pl.ds(pl.multiple_of(blk * tk, tk), tk), :]
        own = lax.broadcasted_iota(jnp.int32, (tk, LANES), 1) < 2 * DIFF_QK
        return [jnp.where(own, kblk, bias_ref[0, 0]), jnp.where(own, bias_ref[0, 1], kblk)]

    def stage(score=None, value=None):
        s_chunks, v_chunks = [], []
        if score is not None:
            s_blk, s_bufs, key_start, s_chunks = score
            lhs = key_lhs(s_blk)
        if value is not None:
            v_blk, v_bufs, v_chunks = value
            vblk = vt_ref[0, v_blk]
            dj = jnp.full((1, cw), v_blk - first_diag, jnp.int32).astype(F32)
        for c in range(max(len(s_chunks), len(v_chunks))):
            if c < len(s_chunks):
                score_chunk(lhs, *s_bufs, *s_chunks[c], key_start)
            if c < len(v_chunks):
                value_chunk(vblk, dj, *v_bufs, *v_chunks[c])

    buf_a, buf_b = (sa_s, tma_s), (sb_s, tmb_s)
    first_diag = (tq // tk) * i
    head = [[c for c in chunks if c[0] == h] for h in range(2)]
    late = [[c for c in head[h] if q_start(c[1]) >= tk] for h in range(2)]
    stage(score=(first_diag + 1, buf_a, tk, late[0]))
    stage(score=(first_diag + 1, buf_a, tk, late[1]), value=(first_diag + 1, buf_a, late[0]))
    stage(score=(first_diag, buf_b, 0, head[0]), value=(first_diag + 1, buf_a, late[1]))
    stage(score=(first_diag, buf_b, 0, head[1]), value=(first_diag, buf_b, head[0]))

    def pair(p, prev):
        stage(score=(2 * p, buf_a, None, head[0]), value=(prev, buf_b, head[1]))
        stage(score=(2 * p, buf_a, None, head[1]), value=(2 * p, buf_a, head[0]))
        stage(score=(2 * p + 1, buf_b, None, head[0]), value=(2 * p, buf_a, head[1]))
        stage(score=(2 * p + 1, buf_b, None, head[1]), value=(2 * p + 1, buf_b, head[0]))
        return 2 * p + 1

    tiles_needed = None
    for h in range(2):
        gap = None
        for mm in range(2):
            idx = 2 * h + mm
            m_now = jnp.maximum(m_s[idx:idx + 1, :], tmb_s[h:h + 1, mm * tq:(mm + 1) * tq])
            g = jnp.max(score_bound[idx] * BOUND_SLACK_REL + BOUND_SLACK_ABS - m_now, axis=1, keepdims=True)
            gap = g if gap is None else jnp.maximum(gap, g)
        need_h = (gap + ZERO_PROB_EXPONENT) / slope_ref[0, h:h + 1, 0:1] + (tk - 1) / tk
        tiles_needed = need_h if tiles_needed is None else jnp.maximum(tiles_needed, need_h)
    n_full = jnp.full((1, 1), first_diag, jnp.int32)
    first_tile = jnp.ceil(n_full.astype(F32) - tiles_needed - 0.01).astype(jnp.int32)
    first_pair = jnp.max(jnp.clip(first_tile, 0, n_full)) // 2

    last = lax.fori_loop(first_pair, i, pair, first_diag)
    stage(value=(last, buf_b, head[1]))

    lamv = lam_ref[...]
    lam = (jnp.exp(jnp.sum(lamv[0:1] * lamv[1:2], axis=1, keepdims=True))
           - jnp.exp(jnp.sum(lamv[2:3] * lamv[3:4], axis=1, keepdims=True)) + LAM_INIT)
    gain = jnp.concatenate([g_ref[...]] * (tq // LANES), axis=1)
    ys = []
    for h in range(2):
        a0 = acc_s[2 * h]
        a1 = acc_s[2 * h + 1]
        a = a0[0:DIFF_V] / a0[DIFF_V:DIFF_V + 1] - lam * (a1[0:DIFF_V] / a1[DIFF_V:DIFF_V + 1])
        ms = jnp.mean(a * a, axis=0, keepdims=True)
        ys.append(a * lax.rsqrt(ms + RMS_EPS) * gain * (1.0 - LAM_INIT))
    o_ref[0] = (jnp.concatenate(ys, axis=0).T * sz_ref[0].astype(F32)).astype(BF16)


def _attn(q, k, vt, kabs, sz, bias, slope_tk, lam_vecs, gain):
    b, s, _ = q.shape
    tq, tk = Q_TILE, K_TILE
    pairs = DIFF_HEADS // 2
    return pl.pallas_call(
        _attn_kernel,
        out_shape=jax.ShapeDtypeStruct((b, s, BRANCH), BF16),
        grid=(b, pairs, s // tq),
        in_specs=[pl.BlockSpec((1, tq, LANES), lambda bi, hp, qi: (bi, qi, hp)),
                  pl.BlockSpec((1, s, LANES), lambda bi, hp, qi: (bi, 0, hp)),
                  pl.BlockSpec((1, s // tk, LANES, tk), lambda bi, hp, qi: (bi, 0, hp, 0)),
                  pl.BlockSpec((1, s // tk, 8, LANES), lambda bi, hp, qi: (bi, 0, 0, hp)),
                  pl.BlockSpec((1, tq, LANES), lambda bi, hp, qi: (bi, qi, hp)),
                  pl.BlockSpec((1, 2, tk, LANES), lambda bi, hp, qi: (hp, 0, 0, 0)),
                  pl.BlockSpec((1, 8, LANES), lambda bi, hp, qi: (hp, 0, 0)),
                  _const_spec((8, LANES), single_buffer=False),
                  _const_spec((DIFF_V, LANES), single_buffer=False)],
        out_specs=pl.BlockSpec((1, tq, LANES), lambda bi, hp, qi: (bi, qi, hp)),
        scratch_shapes=[pltpu.VMEM((2, LANES, 2 * tq), BF16),
                        pltpu.VMEM((8, tq), F32),
                        pltpu.VMEM((4, V_AUG, tq), F32),
                        pltpu.VMEM((2, tk, 2 * tq), F32),
                        pltpu.VMEM((2, tk, 2 * tq), F32),
                        pltpu.VMEM((8, 2 * tq), F32),
                        pltpu.VMEM((8, 2 * tq), F32)],
        compiler_params=pltpu.CompilerParams(dimension_semantics=("parallel", "parallel", "arbitrary"),
                                             vmem_limit_bytes=VMEM_LIMIT),
        name="attn",
    )(q, k, vt, kabs, sz, bias, slope_tk, lam_vecs, gain)


def _epi_kernel(x_ref, odiff_ref, part_ref, gd_ref, lng_ref, lnb_ref, wb1_ref, wout_ref, og_ref, ob_ref, o_ref):
    h = _layer_norm(x_ref[0], lng_ref[...], lnb_ref[...])
    y_diff = jnp.dot(odiff_ref[0], wb1_ref[...], preferred_element_type=F32)
    merged = part_ref[0].astype(F32) + gd_ref[0].astype(F32) * y_diff
    out = jnp.dot(merged.astype(BF16), wout_ref[...], preferred_element_type=F32)
    o_ref[0] = _layer_norm(DEEPNORM_ALPHA * h + out, og_ref[...], ob_ref[...])


def _epi(x, o_diff, part, gd, ln_g, ln_b, wb1, w_out, out_g, out_b):
    b, s, _ = x.shape
    tt = TOKEN_TILE
    tok = lambda width: pl.BlockSpec((1, tt, width), lambda bi, si: (bi, si, 0))
    return pl.pallas_call(
        _epi_kernel,
        out_shape=jax.ShapeDtypeStruct((b, s, D_MODEL), F32),
        grid=(b, s // tt),
        in_specs=[tok(D_MODEL), tok(BRANCH), tok(D_MODEL), tok(D_MODEL),
                  _const_spec((1, D_MODEL)), _const_spec((1, D_MODEL)),
                  _const_spec((BRANCH, D_MODEL)), _const_spec((D_MODEL, D_MODEL)),
                  _const_spec((1, D_MODEL)), _const_spec((1, D_MODEL))],
        out_specs=tok(D_MODEL),
        compiler_params=pltpu.CompilerParams(dimension_semantics=("parallel", "parallel"),
                                             vmem_limit_bytes=VMEM_LIMIT),
        name="epi",
    )(x, o_diff, part, gd, ln_g, ln_b, wb1, w_out, out_g, out_b)


def _alibi_tables():
    pos = np.arange(K_TILE)
    lo = (pos % 256).astype(np.float32)
    hi = (pos - pos % 256).astype(np.float32)
    slopes = [2.0 ** (-8.0 * (h + 1) / DIFF_HEADS) for h in range(DIFF_HEADS)]
    bias = np.zeros((DIFF_HEADS // 2, 2, K_TILE, LANES), np.float32)
    slope_tk = np.zeros((DIFF_HEADS // 2, 8, LANES), np.float32)
    for hp in range(DIFF_HEADS // 2):
        for h in range(2):
            sl = slopes[2 * hp + h]
            for t in range(len(LOG2E_TERMS)):
                bias[hp, h, :, _alibi_lane(h) + 2 * t] = sl * lo
                bias[hp, h, :, _alibi_lane(h) + 2 * t + 1] = sl * hi
            slope_tk[hp, h, :] = sl * K_TILE * LOG2E
    return jnp.asarray(bias, BF16), jnp.asarray(slope_tk, F32)


def kernel(x, mem, ln_in_g, ln_in_b, w_in, b_gate, pool_w, pool_scale, lambda_q1, lambda_k1, lambda_q2, lambda_k2,
           diff_norm_g, w_mem_kv, w_branch, w_out, ln_out_g, ln_out_b):
    assert w_in.shape[0] == DEPTH == 1
    w = w_in[0]
    seg = lambda i: w[:, i * BRANCH:(i + 1) * BRANCH]
    w_main = jnp.concatenate([seg(0), seg(1), seg(2), seg(3), seg(5), seg(6), seg(7)], axis=1).astype(BF16)
    w_vt = seg(4).T.astype(BF16)
    w_g = w[:, 8 * BRANCH:].astype(BF16)
    row = lambda v: v.reshape(1, -1).astype(F32)

    mk, mv = _memkv(mem, w_mem_kv[0].astype(BF16))
    q, k, vt, sz, part, gd, kabs = _proj(
        x, row(ln_in_g), row(ln_in_b), w_main, w_vt, w_g, row(b_gate[0]), pool_w[0].astype(BF16),
        row(pool_scale[0]), mk, mv, w_branch[0, 0].astype(BF16), w_branch[0, 2].astype(BF16))

    bias, slope_tk = _alibi_tables()
    lam_vecs = jnp.concatenate([lambda_q1, lambda_k1, lambda_q2, lambda_k2], axis=0).astype(F32)
    lam_vecs = jnp.pad(lam_vecs, ((0, 4), (0, LANES - DIFF_QK)))
    gain = jnp.broadcast_to(diff_norm_g[0].astype(F32)[:, None], (DIFF_V, LANES))
    o_diff = _attn(q, k, vt, kabs, sz, bias, slope_tk, lam_vecs, gain)

    return _epi(x, o_diff, part, gd, row(ln_in_g), row(ln_in_b), w_branch[0, 1].astype(BF16),
                w_out[0].astype(BF16), row(ln_out_g[0]), row(ln_out_b[0]))
```

```python
import math
import struct

import jax
import jax.numpy as jnp
import numpy as np
from jax import lax
from jax.experimental import pallas as pl
from jax.experimental.pallas import tpu as pltpu

F32 = jnp.float32
BF16 = jnp.bfloat16

D_MODEL = 1024
BRANCH = 512
MEM_LEN = 256
POOL_WINDOWS = (2, 4, 8, 16)
POOL_GROUP = 128
DIFF_HEADS = 8
DIFF_QK = 32
DIFF_V = 64
MEM_HEADS = 4
MEM_HEAD_DIM = 128
LN_EPS = 1e-5
RMS_EPS = 1e-5
DEPTH = 1
DEEPNORM_ALPHA = (2.0 * DEPTH) ** 0.25
LAM_INIT = 0.8 - 0.6 * math.exp(-0.3 * 0)
LOG2E = math.log2(math.e)
DIFF_SCALE = DIFF_QK ** -0.5 * LOG2E


def _bf16_terms(x, n):
    terms = []
    for _ in range(n):
        bits = struct.unpack("<I", struct.pack("<f", x))[0]
        bits = (bits + 0x7FFF + ((bits >> 16) & 1)) & 0xFFFF0000
        t = struct.unpack("<f", struct.pack("<I", bits))[0]
        terms.append(t)
        x -= t
    return terms


LOG2E_TERMS = _bf16_terms(LOG2E, 3)
MEM_SCALE = MEM_HEAD_DIM ** -0.5
NEG_BIG = -1e30
ZERO_PROB_EXPONENT = 151.0
BOUND_SLACK_REL = 1.001
BOUND_SLACK_ABS = 1.0

LANES = 128
BF16_SUBLANES = 16
VMEM_BYTES_V7X = 64 * 1024 * 1024
VMEM_LIMIT = VMEM_BYTES_V7X - 8 * 1024 * 1024

TOKEN_TILE = 512
K_TILE = TOKEN_TILE
Q_TILE = 2 * K_TILE
CHUNK = 256
HALO = 32
V_AUG = DIFF_V + BF16_SUBLANES
ALIBI_COLS = 2 * len(LOG2E_TERMS)
NT_DIMS = (((1,), (1,)), ((), ()))


def _alibi_lane(h):
    return 2 * DIFF_QK * (1 - h)


def _layer_norm(x, g, b):
    mu = jnp.mean(x, axis=-1, keepdims=True)
    xc = x - mu
    var = jnp.mean(xc * xc, axis=-1, keepdims=True)
    return xc * lax.rsqrt(var + LN_EPS) * g + b


def _silu(x):
    return x * jax.nn.sigmoid(x)


def _const_spec(shape, single_buffer=True):
    n = len(shape)
    mode = pl.Buffered(1) if single_buffer else None
    return pl.BlockSpec(shape, lambda *_: (0,) * n, pipeline_mode=mode)


def _memkv_kernel(mem_ref, w_ref, mk_ref, mv_ref):
    kv = jnp.dot(mem_ref[0].astype(BF16), w_ref[...], preferred_element_type=F32)
    mk_ref[0] = kv[:, :BRANCH].astype(BF16)
    mv_ref[0] = kv[:, BRANCH:].astype(BF16)


def _memkv(mem, w_kv):
    b = mem.shape[0]
    return pl.pallas_call(
        _memkv_kernel,
        out_shape=(jax.ShapeDtypeStruct((b, MEM_LEN, BRANCH), BF16),
                   jax.ShapeDtypeStruct((b, MEM_LEN, BRANCH), BF16)),
        grid=(b,),
        in_specs=[pl.BlockSpec((1, MEM_LEN, D_MODEL), lambda i: (i, 0, 0)),
                  _const_spec((D_MODEL, 2 * BRANCH))],
        out_specs=(pl.BlockSpec((1, MEM_LEN, BRANCH), lambda i: (i, 0, 0)),
                   pl.BlockSpec((1, MEM_LEN, BRANCH), lambda i: (i, 0, 0))),
        name="memkv",
    )(mem, w_kv)


def _proj_kernel(x_ref, lng_ref, lnb_ref, wmain_ref, wvt_ref, wg_ref, bg_ref, poolw_ref, pscale_ref,
                 mk_ref, mv_ref, wb0_ref, wb2_ref,
                 q_ref, k_ref, vt_ref, sz_ref, part_ref, gd_ref, kabs_ref,
                 e1_ref, e2_ref, e4_ref, e8_ref):
    tt = TOKEN_TILE
    s_idx = pl.program_id(1)
    h = _layer_norm(x_ref[0], lng_ref[...], lnb_ref[...])
    hb = h.astype(BF16)

    def proj(seg):
        return jnp.dot(hb, wmain_ref[:, seg * BRANCH:(seg + 1) * BRANCH], preferred_element_type=F32)

    u = proj(0)

    @pl.when(s_idx == 0)
    def _():
        e1_ref[0:HALO, :] = jnp.zeros((HALO, BRANCH), F32)

    e1_ref[HALO:HALO + tt, :] = u
    n = HALO + tt
    e2_ref[8:n, :] = e1_ref[8:n, :] + e1_ref[7:n - 1, :]
    e4_ref[16:n, :] = e2_ref[16:n, POOL_GROUP:] + e2_ref[14:n - 2, POOL_GROUP:]
    e8_ref[24:n, :] = e4_ref[24:n, POOL_GROUP:] + e4_ref[20:n - 4, POOL_GROUP:]
    s16 = e8_ref[HALO:n, POOL_GROUP:] + e8_ref[HALO - 8:n - 8, POOL_GROUP:]
    wsum = (e2_ref[HALO:n, 0:POOL_GROUP], e4_ref[HALO:n, 0:POOL_GROUP], e8_ref[HALO:n, 0:POOL_GROUP], s16)
    e1_ref[0:HALO, :] = e1_ref[tt:tt + HALO, :]

    t_pos = s_idx * tt + lax.broadcasted_iota(jnp.int32, (tt, POOL_GROUP), 0)
    pooled = []
    for g, w in enumerate(POOL_WINDOWS):
        cnt = jnp.minimum(t_pos + 1, w).astype(F32)
        pg = wsum[g] / cnt - u[:, g * POOL_GROUP:(g + 1) * POOL_GROUP]
        pooled.append(jnp.dot(pg.astype(BF16), poolw_ref[g], preferred_element_type=F32))
    o_pool = jnp.concatenate(pooled, axis=1) * pscale_ref[...] * _silu(proj(1))

    mq = (proj(5) * MEM_SCALE).astype(BF16)
    heads = []
    for hh in range(MEM_HEADS):
        sl = slice(hh * MEM_HEAD_DIM, (hh + 1) * MEM_HEAD_DIM)
        s = lax.dot_general(mq[:, sl], mk_ref[0, :, sl], NT_DIMS, preferred_element_type=F32)
        p = jnp.exp(s - jnp.max(s, axis=-1, keepdims=True))
        l = jnp.sum(p, axis=-1, keepdims=True)
        heads.append(jnp.dot(p.astype(BF16), mv_ref[0, :, sl], preferred_element_type=F32) / l)
    o_mem = jnp.concatenate(heads, axis=1) * _silu(proj(6))

    def gate(nb):
        sl = slice(nb * D_MODEL, (nb + 1) * D_MODEL)
        return jax.nn.sigmoid(jnp.dot(hb, wg_ref[:, sl], preferred_element_type=F32) + bg_ref[:, sl])

    y_pool = jnp.dot(o_pool.astype(BF16), wb0_ref[...], preferred_element_type=F32)
    y_mem = jnp.dot(o_mem.astype(BF16), wb2_ref[...], preferred_element_type=F32)
    part_ref[0] = (gate(0) * y_pool + gate(2) * y_mem).astype(BF16)
    gd_ref[0] = gate(1).astype(BF16)

    q_ref[0] = (proj(2) * DIFF_SCALE).astype(BF16)
    kb = proj(3).astype(BF16)
    k_ref[0] = kb
    kabs_ref[0, 0] = jnp.broadcast_to(jnp.max(jnp.abs(kb.astype(F32)), axis=0, keepdims=True), (8, BRANCH))
    vt = lax.dot_general(wvt_ref[...], hb, NT_DIMS, preferred_element_type=F32)
    vt_ref[0, 0] = vt.astype(BF16)
    sz_ref[0] = _silu(proj(4)).astype(BF16)


def _proj(x, ln_g, ln_b, w_main, w_vt, w_g, b_g, pool_w, pool_scale, mk, mv, wb0, wb2):
    b, s, _ = x.shape
    tt = TOKEN_TILE
    ns = s // tt
    tok = lambda width: pl.BlockSpec((1, tt, width), lambda bi, si: (bi, si, 0))
    per_batch = pl.BlockSpec((1, MEM_LEN, BRANCH), lambda bi, si: (bi, 0, 0))
    return pl.pallas_call(
        _proj_kernel,
        out_shape=(jax.ShapeDtypeStruct((b, s, BRANCH), BF16),
                   jax.ShapeDtypeStruct((b, s, BRANCH), BF16),
                   jax.ShapeDtypeStruct((b, ns, BRANCH, tt), BF16),
                   jax.ShapeDtypeStruct((b, s, BRANCH), BF16),
                   jax.ShapeDtypeStruct((b, s, D_MODEL), BF16),
                   jax.ShapeDtypeStruct((b, s, D_MODEL), BF16),
                   jax.ShapeDtypeStruct((b, ns, 8, BRANCH), F32)),
        grid=(b, ns),
        in_specs=[tok(D_MODEL),
                  _const_spec((1, D_MODEL)), _const_spec((1, D_MODEL)),
                  _const_spec((D_MODEL, 7 * BRANCH)), _const_spec((BRANCH, D_MODEL)),
                  _const_spec((D_MODEL, 3 * D_MODEL)), _const_spec((1, 3 * D_MODEL)),
                  _const_spec((len(POOL_WINDOWS), POOL_GROUP, POOL_GROUP)), _const_spec((1, BRANCH)),
                  per_batch, per_batch,
                  _const_spec((BRANCH, D_MODEL)), _const_spec((BRANCH, D_MODEL))],
        out_specs=(tok(BRANCH), tok(BRANCH),
                   pl.BlockSpec((1, 1, BRANCH, tt), lambda bi, si: (bi, si, 0, 0)),
                   tok(BRANCH), tok(D_MODEL), tok(D_MODEL),
                   pl.BlockSpec((1, 1, 8, BRANCH), lambda bi, si: (bi, si, 0, 0))),
        scratch_shapes=[pltpu.VMEM((HALO + tt, BRANCH), F32),
                        pltpu.VMEM((HALO + tt, BRANCH), F32),
                        pltpu.VMEM((HALO + tt, BRANCH - POOL_GROUP), F32),
                        pltpu.VMEM((HALO + tt, BRANCH - 2 * POOL_GROUP), F32)],
        compiler_params=pltpu.CompilerParams(dimension_semantics=("arbitrary", "arbitrary"),
                                             vmem_limit_bytes=VMEM_LIMIT),
        name="proj",
    )(x, ln_g, ln_b, w_main, w_vt, w_g, b_g, pool_w, pool_scale, mk, mv, wb0, wb2)


def _attn_kernel(q_ref, k_ref, vt_ref, kabs_ref, sz_ref, bias_ref, slope_ref, lam_ref, g_ref, o_ref,
                 qt_s, m_s, acc_s,
                 sa_s, sb_s, tma_s, tmb_s):
    tq, tk = Q_TILE, K_TILE
    i = pl.program_id(2)

    qt32 = q_ref[0].astype(F32).T
    qt = qt32.astype(BF16)
    kcol = jnp.max(kabs_ref[0], axis=0).T[:, 0:1]
    qk_bound = jnp.abs(qt32) * kcol
    score_bound = [jnp.sum(qk_bound[DIFF_QK * idx:DIFF_QK * (idx + 1)], axis=0, keepdims=True)
                   for idx in range(4)]

    @pl.when(i == 0)
    def _():
        row = lax.broadcasted_iota(jnp.int32, (BF16_SUBLANES, 2 * tq), 0)
        for h in range(2):
            qt_s[h] = jnp.zeros((LANES, 2 * tq), BF16)
            alibi_rows = jnp.zeros((BF16_SUBLANES, 2 * tq), F32)
            for t, c in enumerate(LOG2E_TERMS):
                alibi_rows = jnp.where((row == 2 * t) | (row == 2 * t + 1), c, alibi_rows)
            qt_s[h, _alibi_lane(h):_alibi_lane(h) + BF16_SUBLANES, :] = alibi_rows.astype(BF16)

    for h in range(2):
        base = 2 * DIFF_QK * h
        qt_s[h, base:base + DIFF_QK, 0:tq] = qt[base:base + DIFF_QK]
        qt_s[h, base + DIFF_QK:base + 2 * DIFF_QK, tq:2 * tq] = qt[base + DIFF_QK:base + 2 * DIFF_QK]
    m_s[...] = jnp.full(m_s.shape, NEG_BIG, F32)
    acc_s[...] = jnp.zeros(acc_s.shape, F32)

    ones_rows = jnp.where(lax.broadcasted_iota(jnp.int32, (BF16_SUBLANES, tk), 0) == 0, 1.0, 0.0).astype(BF16)
    slope_tk = [jnp.concatenate([slope_ref[0, h:h + 1, :]] * (tq // LANES), axis=1) for h in range(2)]

    cw = CHUNK
    chunks = [(h, n) for h in range(2) for n in range(2 * tq // cw)]

    def q_start(n):
        return (n * cw) % tq

    def live_rows(n, key_start):
        return tk if key_start is None else min(tk, q_start(n) + cw - key_start)

    def score_chunk(lhs, s_buf, tm_buf, h, n, key_start):
        cols = slice(n * cw, (n + 1) * cw)
        kr = live_rows(n, key_start)
        st = jnp.dot(lhs[h][:kr], qt_s[h, :, cols], preferred_element_type=F32)
        if key_start is not None and key_start + kr - 1 > q_start(n):
            kk = lax.broadcasted_iota(jnp.int32, (kr, cw), 0) + key_start
            qq = lax.broadcasted_iota(jnp.int32, (kr, cw), 1) + q_start(n)
            st = jnp.where(kk <= qq, st, NEG_BIG)
        s_buf[h, :kr, cols] = st
        if kr < tk:
            s_buf[h, kr:, cols] = jnp.full((tk - kr, cw), NEG_BIG, F32)
        tm_buf[h:h + 1, cols] = jnp.max(st, axis=0, keepdims=True)

    def value_chunk(vblk, dj, s_buf, tm_buf, h, n, key_start):
        idx = 2 * h + (n * cw) // tq
        cols = slice(n * cw, (n + 1) * cw)
        qcols = slice(q_start(n), q_start(n) + cw)
        kr = live_rows(n, key_start)
        off = slope_tk[h][:, qcols] * dj
        vaug = jnp.concatenate([vblk[DIFF_V * h:DIFF_V * (h + 1)], ones_rows], axis=0)
        m_old = m_s[idx:idx + 1, qcols]
        m_new = jnp.maximum(m_old, tm_buf[h:h + 1, cols] + off)
        alpha = jnp.exp2(m_old - m_new)
        p = jnp.exp2(s_buf[h, :kr, cols] - (m_new - off)).astype(BF16)
        pv = jnp.dot(vaug[:, :kr], p, preferred_element_type=F32)
        acc_s[idx, :, qcols] = alpha * acc_s[idx, :, qcols] + pv
        m_s[idx:idx + 1, qcols] = m_new

    def key_lhs(blk):
        kblk = k_ref[0, pl.ds(pl.multiple_of(blk * tk, tk), tk), :]
        own = lax.broadcasted_iota(jnp.int32, (tk, LANES), 1) < 2 * DIFF_QK
        return [jnp.where(own, kblk, bias_ref[0, 0]), jnp.where(own, bias_ref[0, 1], kblk)]

    def stage(score=None, value=None):
        s_chunks, v_chunks = [], []
        if score is not None:
            s_blk, s_bufs, s_start, s_chunks = score
            lhs = key_lhs(s_blk)
        if value is not None:
            v_blk, v_bufs, v_start, v_chunks = value
            vblk = vt_ref[0, v_blk]
            dj = jnp.full((1, cw), v_blk - first_diag, jnp.int32).astype(F32)
        for c in range(max(len(s_chunks), len(v_chunks))):
            if c < len(s_chunks):
                score_chunk(lhs, *s_bufs, *s_chunks[c], s_start)
            if c < len(v_chunks):
                value_chunk(vblk, dj, *v_bufs, *v_chunks[c], v_start)

    buf_a, buf_b = (sa_s, tma_s), (sb_s, tmb_s)
    first_diag = (tq // tk) * i
    head = [[c for c in chunks if c[0] == h] for h in range(2)]
    late = [[c for c in head[h] if q_start(c[1]) >= tk] for h in range(2)]
    stage(score=(first_diag + 1, buf_a, tk, late[0]))
    stage(score=(first_diag + 1, buf_a, tk, late[1]), value=(first_diag + 1, buf_a, tk, late[0]))
    stage(score=(first_diag, buf_b, 0, head[0]), value=(first_diag + 1, buf_a, tk, late[1]))
    stage(score=(first_diag, buf_b, 0, head[1]), value=(first_diag, buf_b, 0, head[0]))

    def pair(p, prev):
        stage(score=(2 * p, buf_a, None, head[0]), value=(prev, buf_b, None, head[1]))
        stage(score=(2 * p, buf_a, None, head[1]), value=(2 * p, buf_a, None, head[0]))
        stage(score=(2 * p + 1, buf_b, None, head[0]), value=(2 * p, buf_a, None, head[1]))
        stage(score=(2 * p + 1, buf_b, None, head[1]), value=(2 * p + 1, buf_b, None, head[0]))
        return 2 * p + 1

    tiles_needed = None
    for h in range(2):
        gap = None
        for mm in range(2):
            idx = 2 * h + mm
            m_now = jnp.maximum(m_s[idx:idx + 1, :], tmb_s[h:h + 1, mm * tq:(mm + 1) * tq])
            g = jnp.max(score_bound[idx] * BOUND_SLACK_REL + BOUND_SLACK_ABS - m_now, axis=1, keepdims=True)
            gap = g if gap is None else jnp.maximum(gap, g)
        need_h = (gap + ZERO_PROB_EXPONENT) / slope_ref[0, h:h + 1, 0:1] + (tk - 1) / tk
        tiles_needed = need_h if tiles_needed is None else jnp.maximum(tiles_needed, need_h)
    n_full = jnp.full((1, 1), first_diag, jnp.int32)
    first_tile = jnp.ceil(n_full.astype(F32) - tiles_needed - 0.01).astype(jnp.int32)
    first_pair = jnp.max(jnp.clip(first_tile, 0, n_full)) // 2

    last = lax.fori_loop(first_pair, i, pair, first_diag)
    stage(value=(last, buf_b, None, head[1]))

    lamv = lam_ref[...]
    lam = (jnp.exp(jnp.sum(lamv[0:1] * lamv[1:2], axis=1, keepdims=True))
           - jnp.exp(jnp.sum(lamv[2:3] * lamv[3:4], axis=1, keepdims=True)) + LAM_INIT)
    gain = jnp.concatenate([g_ref[...]] * (tq // LANES), axis=1)
    ys = []
    for h in range(2):
        a0 = acc_s[2 * h]
        a1 = acc_s[2 * h + 1]
        a = a0[0:DIFF_V] / a0[DIFF_V:DIFF_V + 1] - lam * (a1[0:DIFF_V] / a1[DIFF_V:DIFF_V + 1])
        ms = jnp.mean(a * a, axis=0, keepdims=True)
        ys.append(a * lax.rsqrt(ms + RMS_EPS) * gain * (1.0 - LAM_INIT))
    o_ref[0] = (jnp.concatenate(ys, axis=0).T * sz_ref[0].astype(F32)).astype(BF16)


def _attn(q, k, vt, kabs, sz, bias, slope_tk, lam_vecs, gain):
    b, s, _ = q.shape
    tq, tk = Q_TILE, K_TILE
    pairs = DIFF_HEADS // 2
    return pl.pallas_call(
        _attn_kernel,
        out_shape=jax.ShapeDtypeStruct((b, s, BRANCH), BF16),
        grid=(b, pairs, s // tq),
        in_specs=[pl.BlockSpec((1, tq, LANES), lambda bi, hp, qi: (bi, qi, hp)),
                  pl.BlockSpec((1, s, LANES), lambda bi, hp, qi: (bi, 0, hp)),
                  pl.BlockSpec((1, s // tk, LANES, tk), lambda bi, hp, qi: (bi, 0, hp, 0)),
                  pl.BlockSpec((1, s // tk, 8, LANES), lambda bi, hp, qi: (bi, 0, 0, hp)),
                  pl.BlockSpec((1, tq, LANES), lambda bi, hp, qi: (bi, qi, hp)),
                  pl.BlockSpec((1, 2, tk, LANES), lambda bi, hp, qi: (hp, 0, 0, 0)),
                  pl.BlockSpec((1, 8, LANES), lambda bi, hp, qi: (hp, 0, 0)),
                  _const_spec((8, LANES), single_buffer=False),
                  _const_spec((DIFF_V, LANES), single_buffer=False)],
        out_specs=pl.BlockSpec((1, tq, LANES), lambda bi, hp, qi: (bi, qi, hp)),
        scratch_shapes=[pltpu.VMEM((2, LANES, 2 * tq), BF16),
                        pltpu.VMEM((8, tq), F32),
                        pltpu.VMEM((4, V_AUG, tq), F32),
                        pltpu.VMEM((2, tk, 2 * tq), F32),
                        pltpu.VMEM((2, tk, 2 * tq), F32),
                        pltpu.VMEM((8, 2 * tq), F32),
                        pltpu.VMEM((8, 2 * tq), F32)],
        compiler_params=pltpu.CompilerParams(dimension_semantics=("parallel", "parallel", "arbitrary"),
                                             vmem_limit_bytes=VMEM_LIMIT),
        name="attn",
    )(q, k, vt, kabs, sz, bias, slope_tk, lam_vecs, gain)


def _epi_kernel(x_ref, odiff_ref, part_ref, gd_ref, lng_ref, lnb_ref, wb1_ref, wout_ref, og_ref, ob_ref, o_ref):
    h = _layer_norm(x_ref[0], lng_ref[...], lnb_ref[...])
    y_diff = jnp.dot(odiff_ref[0], wb1_ref[...], preferred_element_type=F32)
    merged = part_ref[0].astype(F32) + gd_ref[0].astype(F32) * y_diff
    out = jnp.dot(merged.astype(BF16), wout_ref[...], preferred_element_type=F32)
    o_ref[0] = _layer_norm(DEEPNORM_ALPHA * h + out, og_ref[...], ob_ref[...])


def _epi(x, o_diff, part, gd, ln_g, ln_b, wb1, w_out, out_g, out_b):
    b, s, _ = x.shape
    tt = TOKEN_TILE
    tok = lambda width: pl.BlockSpec((1, tt, width), lambda bi, si: (bi, si, 0))
    return pl.pallas_call(
        _epi_kernel,
        out_shape=jax.ShapeDtypeStruct((b, s, D_MODEL), F32),
        grid=(b, s // tt),
        in_specs=[tok(D_MODEL), tok(BRANCH), tok(D_MODEL), tok(D_MODEL),
                  _const_spec((1, D_MODEL)), _const_spec((1, D_MODEL)),
                  _const_spec((BRANCH, D_MODEL)), _const_spec((D_MODEL, D_MODEL)),
                  _const_spec((1, D_MODEL)), _const_spec((1, D_MODEL))],
        out_specs=tok(D_MODEL),
        compiler_params=pltpu.CompilerParams(dimension_semantics=("parallel", "parallel"),
                                             vmem_limit_bytes=VMEM_LIMIT),
        name="epi",
    )(x, o_diff, part, gd, ln_g, ln_b, wb1, w_out, out_g, out_b)


def _alibi_tables():
    pos = np.arange(K_TILE)
    lo = (pos % 256).astype(np.float32)
    hi = (pos - pos % 256).astype(np.float32)
    slopes = [2.0 ** (-8.0 * (h + 1) / DIFF_HEADS) for h in range(DIFF_HEADS)]
    bias = np.zeros((DIFF_HEADS // 2, 2, K_TILE, LANES), np.float32)
    slope_tk = np.zeros((DIFF_HEADS // 2, 8, LANES), np.float32)
    for hp in range(DIFF_HEADS // 2):
        for h in range(2):
            sl = slopes[2 * hp + h]
            for t in range(len(LOG2E_TERMS)):
                bias[hp, h, :, _alibi_lane(h) + 2 * t] = sl * lo
                bias[hp, h, :, _alibi_lane(h) + 2 * t + 1] = sl * hi
            slope_tk[hp, h, :] = sl * K_TILE * LOG2E
    return jnp.asarray(bias, BF16), jnp.asarray(slope_tk, F32)


def kernel(x, mem, ln_in_g, ln_in_b, w_in, b_gate, pool_w, pool_scale, lambda_q1, lambda_k1, lambda_q2, lambda_k2,
           diff_norm_g, w_mem_kv, w_branch, w_out, ln_out_g, ln_out_b):
    assert w_in.shape[0] == DEPTH == 1
    assert x.shape[2] == D_MODEL and x.shape[1] % Q_TILE == 0 and mem.shape[1:] == (MEM_LEN, D_MODEL)
    assert Q_TILE == 2 * K_TILE and K_TILE % CHUNK == 0 and K_TILE <= 512
    w = w_in[0]
    seg = lambda i: w[:, i * BRANCH:(i + 1) * BRANCH]
    w_main = jnp.concatenate([seg(0), seg(1), seg(2), seg(3), seg(5), seg(6), seg(7)], axis=1).astype(BF16)
    w_vt = seg(4).T.astype(BF16)
    w_g = w[:, 8 * BRANCH:].astype(BF16)
    row = lambda v: v.reshape(1, -1).astype(F32)

    mk, mv = _memkv(mem, w_mem_kv[0].astype(BF16))
    q, k, vt, sz, part, gd, kabs = _proj(
        x, row(ln_in_g), row(ln_in_b), w_main, w_vt, w_g, row(b_gate[0]), pool_w[0].astype(BF16),
        row(pool_scale[0]), mk, mv, w_branch[0, 0].astype(BF16), w_branch[0, 2].astype(BF16))

    bias, slope_tk = _alibi_tables()
    lam_vecs = jnp.concatenate([lambda_q1, lambda_k1, lambda_q2, lambda_k2], axis=0).astype(F32)
    lam_vecs = jnp.pad(lam_vecs, ((0, 4), (0, LANES - DIFF_QK)))
    gain = jnp.broadcast_to(diff_norm_g[0].astype(F32)[:, None], (DIFF_V, LANES))
    o_diff = _attn(q, k, vt, kabs, sz, bias, slope_tk, lam_vecs, gain)

    return _epi(x, o_diff, part, gd, row(ln_in_g), row(ln_in_b), w_branch[0, 1].astype(BF16),
                w_out[0].astype(BF16), row(ln_out_g[0]), row(ln_out_b[0]))
```

```python
import math
import struct

import jax
import jax.numpy as jnp
import numpy as np
from jax import lax
from jax.experimental import pallas as pl
from jax.experimental.pallas import tpu as pltpu

F32 = jnp.float32
BF16 = jnp.bfloat16

D_MODEL = 1024
BRANCH = 512
MEM_LEN = 256
POOL_WINDOWS = (2, 4, 8, 16)
POOL_GROUP = 128
DIFF_HEADS = 8
DIFF_QK = 32
DIFF_V = 64
MEM_HEADS = 4
MEM_HEAD_DIM = 128
LN_EPS = 1e-5
RMS_EPS = 1e-5
DEPTH = 1
DEEPNORM_ALPHA = (2.0 * DEPTH) ** 0.25
LAM_INIT = 0.8 - 0.6 * math.exp(-0.3 * 0)
LOG2E = math.log2(math.e)
DIFF_SCALE = DIFF_QK ** -0.5 * LOG2E


def _bf16_terms(x, n):
    terms = []
    for _ in range(n):
        bits = struct.unpack("<I", struct.pack("<f", x))[0]
        bits = (bits + 0x7FFF + ((bits >> 16) & 1)) & 0xFFFF0000
        t = struct.unpack("<f", struct.pack("<I", bits))[0]
        terms.append(t)
        x -= t
    return terms


LOG2E_TERMS = _bf16_terms(LOG2E, 3)
MEM_SCALE = MEM_HEAD_DIM ** -0.5
NEG_BIG = -1e30
ZERO_PROB_EXPONENT = 151.0
BOUND_SLACK_REL = 1.001
BOUND_SLACK_ABS = 1.0
NORM_SLACK_REL = 1.01

LANES = 128
BF16_SUBLANES = 16
VMEM_BYTES_V7X = 64 * 1024 * 1024
VMEM_LIMIT = VMEM_BYTES_V7X - 8 * 1024 * 1024

TOKEN_TILE = 512
K_TILE = TOKEN_TILE
Q_TILE = 2 * K_TILE
CHUNK = 256
HALO = 32
V_AUG = DIFF_V + BF16_SUBLANES
ALIBI_COLS = 2 * len(LOG2E_TERMS)
NT_DIMS = (((1,), (1,)), ((), ()))


def _alibi_lane(h):
    return 2 * DIFF_QK * (1 - h)


def _layer_norm(x, g, b):
    mu = jnp.mean(x, axis=-1, keepdims=True)
    xc = x - mu
    var = jnp.mean(xc * xc, axis=-1, keepdims=True)
    return xc * lax.rsqrt(var + LN_EPS) * g + b


def _silu(x):
    return x * jax.nn.sigmoid(x)


def _const_spec(shape, single_buffer=True):
    n = len(shape)
    mode = pl.Buffered(1) if single_buffer else None
    return pl.BlockSpec(shape, lambda *_: (0,) * n, pipeline_mode=mode)


def _memkv_kernel(mem_ref, w_ref, mk_ref, mv_ref):
    kv = jnp.dot(mem_ref[0].astype(BF16), w_ref[...], preferred_element_type=F32)
    mk_ref[0] = kv[:, :BRANCH].astype(BF16)
    mv_ref[0] = kv[:, BRANCH:].astype(BF16)


def _memkv(mem, w_kv):
    b = mem.shape[0]
    return pl.pallas_call(
        _memkv_kernel,
        out_shape=(jax.ShapeDtypeStruct((b, MEM_LEN, BRANCH), BF16),
                   jax.ShapeDtypeStruct((b, MEM_LEN, BRANCH), BF16)),
        grid=(b,),
        in_specs=[pl.BlockSpec((1, MEM_LEN, D_MODEL), lambda i: (i, 0, 0)),
                  _const_spec((D_MODEL, 2 * BRANCH))],
        out_specs=(pl.BlockSpec((1, MEM_LEN, BRANCH), lambda i: (i, 0, 0)),
                   pl.BlockSpec((1, MEM_LEN, BRANCH), lambda i: (i, 0, 0))),
        name="memkv",
    )(mem, w_kv)


def _proj_kernel(x_ref, lng_ref, lnb_ref, wmain_ref, wvt_ref, wg_ref, bg_ref, poolw_ref, pscale_ref,
                 mk_ref, mv_ref, wb0_ref, wb2_ref, segsum_ref,
                 q_ref, k_ref, vt_ref, sz_ref, part_ref, gd_ref, kabs_ref,
                 e1_ref, e2_ref, e4_ref, e8_ref):
    tt = TOKEN_TILE
    s_idx = pl.program_id(1)
    h = _layer_norm(x_ref[0], lng_ref[...], lnb_ref[...])
    hb = h.astype(BF16)

    def proj(seg):
        return jnp.dot(hb, wmain_ref[:, seg * BRANCH:(seg + 1) * BRANCH], preferred_element_type=F32)

    u = proj(0)

    @pl.when(s_idx == 0)
    def _():
        e1_ref[0:HALO, :] = jnp.zeros((HALO, BRANCH), F32)

    e1_ref[HALO:HALO + tt, :] = u
    n = HALO + tt
    e2_ref[8:n, :] = e1_ref[8:n, :] + e1_ref[7:n - 1, :]
    e4_ref[16:n, :] = e2_ref[16:n, POOL_GROUP:] + e2_ref[14:n - 2, POOL_GROUP:]
    e8_ref[24:n, :] = e4_ref[24:n, POOL_GROUP:] + e4_ref[20:n - 4, POOL_GROUP:]
    s16 = e8_ref[HALO:n, POOL_GROUP:] + e8_ref[HALO - 8:n - 8, POOL_GROUP:]
    wsum = (e2_ref[HALO:n, 0:POOL_GROUP], e4_ref[HALO:n, 0:POOL_GROUP], e8_ref[HALO:n, 0:POOL_GROUP], s16)
    e1_ref[0:HALO, :] = e1_ref[tt:tt + HALO, :]

    t_pos = s_idx * tt + lax.broadcasted_iota(jnp.int32, (tt, POOL_GROUP), 0)
    pooled = []
    for g, w in enumerate(POOL_WINDOWS):
        cnt = jnp.minimum(t_pos + 1, w).astype(F32)
        pg = wsum[g] / cnt - u[:, g * POOL_GROUP:(g + 1) * POOL_GROUP]
        pooled.append(jnp.dot(pg.astype(BF16), poolw_ref[g], preferred_element_type=F32))
    o_pool = jnp.concatenate(pooled, axis=1) * pscale_ref[...] * _silu(proj(1))

    mq = (proj(5) * MEM_SCALE).astype(BF16)
    heads = []
    for hh in range(MEM_HEADS):
        sl = slice(hh * MEM_HEAD_DIM, (hh + 1) * MEM_HEAD_DIM)
        s = lax.dot_general(mq[:, sl], mk_ref[0, :, sl], NT_DIMS, preferred_element_type=F32)
        p = jnp.exp(s - jnp.max(s, axis=-1, keepdims=True))
        l = jnp.sum(p, axis=-1, keepdims=True)
        heads.append(jnp.dot(p.astype(BF16), mv_ref[0, :, sl], preferred_element_type=F32) / l)
    o_mem = jnp.concatenate(heads, axis=1) * _silu(proj(6))

    def gate(nb):
        sl = slice(nb * D_MODEL, (nb + 1) * D_MODEL)
        return jax.nn.sigmoid(jnp.dot(hb, wg_ref[:, sl], preferred_element_type=F32) + bg_ref[:, sl])

    y_pool = jnp.dot(o_pool.astype(BF16), wb0_ref[...], preferred_element_type=F32)
    y_mem = jnp.dot(o_mem.astype(BF16), wb2_ref[...], preferred_element_type=F32)
    part_ref[0] = (gate(0) * y_pool + gate(2) * y_mem).astype(BF16)
    gd_ref[0] = gate(1).astype(BF16)

    q_ref[0] = (proj(2) * DIFF_SCALE).astype(BF16)
    kb = proj(3).astype(BF16)
    k_ref[0] = kb
    kf = kb.astype(F32)
    norm2 = jnp.dot((kf * kf).astype(BF16), segsum_ref[...], preferred_element_type=F32)
    kabs_ref[0, 0] = jnp.concatenate(
        [jnp.max(jnp.abs(kf), axis=0, keepdims=True),
         jnp.concatenate([jnp.max(norm2, axis=0, keepdims=True)] * (BRANCH // LANES), axis=1),
         jnp.zeros((6, BRANCH), F32)], axis=0)
    vt = lax.dot_general(wvt_ref[...], hb, NT_DIMS, preferred_element_type=F32)
    vt_ref[0, 0] = vt.astype(BF16)
    sz_ref[0] = _silu(proj(4)).astype(BF16)


def _proj(x, ln_g, ln_b, w_main, w_vt, w_g, b_g, pool_w, pool_scale, mk, mv, wb0, wb2):
    b, s, _ = x.shape
    tt = TOKEN_TILE
    ns = s // tt
    tok = lambda width: pl.BlockSpec((1, tt, width), lambda bi, si: (bi, si, 0))
    per_batch = pl.BlockSpec((1, MEM_LEN, BRANCH), lambda bi, si: (bi, 0, 0))
    segsum = jnp.asarray(np.arange(BRANCH)[:, None] // DIFF_QK == np.arange(LANES)[None, :], BF16)
    return pl.pallas_call(
        _proj_kernel,
        out_shape=(jax.ShapeDtypeStruct((b, s, BRANCH), BF16),
                   jax.ShapeDtypeStruct((b, s, BRANCH), BF16),
                   jax.ShapeDtypeStruct((b, ns, BRANCH, tt), BF16),
                   jax.ShapeDtypeStruct((b, s, BRANCH), BF16),
                   jax.ShapeDtypeStruct((b, s, D_MODEL), BF16),
                   jax.ShapeDtypeStruct((b, s, D_MODEL), BF16),
                   jax.ShapeDtypeStruct((b, ns, 8, BRANCH), F32)),
        grid=(b, ns),
        in_specs=[tok(D_MODEL),
                  _const_spec((1, D_MODEL)), _const_spec((1, D_MODEL)),
                  _const_spec((D_MODEL, 7 * BRANCH)), _const_spec((BRANCH, D_MODEL)),
                  _const_spec((D_MODEL, 3 * D_MODEL)), _const_spec((1, 3 * D_MODEL)),
                  _const_spec((len(POOL_WINDOWS), POOL_GROUP, POOL_GROUP)), _const_spec((1, BRANCH)),
                  per_batch, per_batch,
                  _const_spec((BRANCH, D_MODEL)), _const_spec((BRANCH, D_MODEL)),
                  _const_spec((BRANCH, LANES))],
        out_specs=(tok(BRANCH), tok(BRANCH),
                   pl.BlockSpec((1, 1, BRANCH, tt), lambda bi, si: (bi, si, 0, 0)),
                   tok(BRANCH), tok(D_MODEL), tok(D_MODEL),
                   pl.BlockSpec((1, 1, 8, BRANCH), lambda bi, si: (bi, si, 0, 0))),
        scratch_shapes=[pltpu.VMEM((HALO + tt, BRANCH), F32),
                        pltpu.VMEM((HALO + tt, BRANCH), F32),
                        pltpu.VMEM((HALO + tt, BRANCH - POOL_GROUP), F32),
                        pltpu.VMEM((HALO + tt, BRANCH - 2 * POOL_GROUP), F32)],
        compiler_params=pltpu.CompilerParams(dimension_semantics=("arbitrary", "arbitrary"),
                                             vmem_limit_bytes=VMEM_LIMIT),
        name="proj",
    )(x, ln_g, ln_b, w_main, w_vt, w_g, b_g, pool_w, pool_scale, mk, mv, wb0, wb2, segsum)


def _attn_kernel(q_ref, k_ref, vt_ref, kabs_ref, sz_ref, bias_ref, slope_ref, lam_ref, g_ref, o_ref,
                 qt_s, m_s, acc_s,
                 sa_s, sb_s, tma_s, tmb_s):
    tq, tk = Q_TILE, K_TILE
    i = pl.program_id(2)

    qt32 = q_ref[0].astype(F32).T
    qt = qt32.astype(BF16)
    kstat = jnp.max(kabs_ref[0], axis=0)
    kcol = kstat.T[:, 0:1]
    qk_bound = jnp.abs(qt32) * kcol
    q_sq = qt32 * qt32
    lane = lax.broadcasted_iota(jnp.int32, (1, LANES), 1)
    score_bound = []
    for idx in range(4):
        rows = slice(DIFF_QK * idx, DIFF_QK * (idx + 1))
        k_sq = jnp.max(jnp.where(lane == 4 * pl.program_id(1) + idx, kstat[1:2], 0.0), axis=1, keepdims=True)
        by_norm = jnp.sqrt(jnp.sum(q_sq[rows], axis=0, keepdims=True) * (k_sq * NORM_SLACK_REL))
        score_bound.append(jnp.minimum(jnp.sum(qk_bound[rows], axis=0, keepdims=True), by_norm))

    @pl.when(i == 0)
    def _():
        row = lax.broadcasted_iota(jnp.int32, (BF16_SUBLANES, 2 * tq), 0)
        for h in range(2):
            qt_s[h] = jnp.zeros((LANES, 2 * tq), BF16)
            alibi_rows = jnp.zeros((BF16_SUBLANES, 2 * tq), F32)
            for t, c in enumerate(LOG2E_TERMS):
                alibi_rows = jnp.where((row == 2 * t) | (row == 2 * t + 1), c, alibi_rows)
            qt_s[h, _alibi_lane(h):_alibi_lane(h) + BF16_SUBLANES, :] = alibi_rows.astype(BF16)

    for h in range(2):
        base = 2 * DIFF_QK * h
        qt_s[h, base:base + DIFF_QK, 0:tq] = qt[base:base + DIFF_QK]
        qt_s[h, base + DIFF_QK:base + 2 * DIFF_QK, tq:2 * tq] = qt[base + DIFF_QK:base + 2 * DIFF_QK]
    m_s[...] = jnp.full(m_s.shape, NEG_BIG, F32)
    acc_s[...] = jnp.zeros(acc_s.shape, F32)

    ones_rows = jnp.where(lax.broadcasted_iota(jnp.int32, (BF16_SUBLANES, tk), 0) == 0, 1.0, 0.0).astype(BF16)
    slope_tk = [jnp.concatenate([slope_ref[0, h:h + 1, :]] * (tq // LANES), axis=1) for h in range(2)]

    cw = CHUNK
    chunks = [(h, n) for h in range(2) for n in range(2 * tq // cw)]

    def q_start(n):
        return (n * cw) % tq

    def live_rows(n, key_start):
        return tk if key_start is None else min(tk, q_start(n) + cw - key_start)

    def score_chunk(lhs, s_buf, tm_buf, h, n, key_start):
        cols = slice(n * cw, (n + 1) * cw)
        kr = live_rows(n, key_start)
        st = jnp.dot(lhs[h][:kr], qt_s[h, :, cols], preferred_element_type=F32)
        if key_start is not None and key_start + kr - 1 > q_start(n):
            kk = lax.broadcasted_iota(jnp.int32, (kr, cw), 0) + key_start
            qq = lax.broadcasted_iota(jnp.int32, (kr, cw), 1) + q_start(n)
            st = jnp.where(kk <= qq, st, NEG_BIG)
        s_buf[h, :kr, cols] = st
        if kr < tk:
            s_buf[h, kr:, cols] = jnp.full((tk - kr, cw), NEG_BIG, F32)
        tm_buf[h:h + 1, cols] = jnp.max(st, axis=0, keepdims=True)

    def value_chunk(vblk, dj, s_buf, tm_buf, h, n, key_start):
        idx = 2 * h + (n * cw) // tq
        cols = slice(n * cw, (n + 1) * cw)
        qcols = slice(q_start(n), q_start(n) + cw)
        kr = live_rows(n, key_start)
        off = slope_tk[h][:, qcols] * dj
        vaug = jnp.concatenate([vblk[DIFF_V * h:DIFF_V * (h + 1)], ones_rows], axis=0)
        m_old = m_s[idx:idx + 1, qcols]
        m_new = jnp.maximum(m_old, tm_buf[h:h + 1, cols] + off)
        alpha = jnp.exp2(m_old - m_new)
        p = jnp.exp2(s_buf[h, :kr, cols] - (m_new - off)).astype(BF16)
        pv = jnp.dot(vaug[:, :kr], p, preferred_element_type=F32)
        acc_s[idx, :, qcols] = alpha * acc_s[idx, :, qcols] + pv
        m_s[idx:idx + 1, qcols] = m_new

    def key_lhs(blk):
        kblk = k_ref[0, pl.ds(pl.multiple_of(blk * tk, tk), tk), :]
        own = lax.broadcasted_iota(jnp.int32, (tk, LANES), 1) < 2 * DIFF_QK
        return [jnp.where(own, kblk, bias_ref[0, 0]), jnp.where(own, bias_ref[0, 1], kblk)]

    def stage(score=None, value=None):
        s_chunks, v_chunks = [], []
        if score is not None:
            s_blk, s_bufs, s_start, s_chunks = score
            lhs = key_lhs(s_blk)
        if value is not None:
            v_blk, v_bufs, v_start, v_chunks = value
            vblk = vt_ref[0, v_blk]
            dj = jnp.full((1, cw), v_blk - first_diag, jnp.int32).astype(F32)
        for c in range(max(len(s_chunks), len(v_chunks))):
            if c < len(s_chunks):
                score_chunk(lhs, *s_bufs, *s_chunks[c], s_start)
            if c < len(v_chunks):
                value_chunk(vblk, dj, *v_bufs, *v_chunks[c], v_start)

    buf_a, buf_b = (sa_s, tma_s), (sb_s, tmb_s)
    first_diag = (tq // tk) * i
    head = [[c for c in chunks if c[0] == h] for h in range(2)]
    late = [[c for c in head[h] if q_start(c[1]) >= tk] for h in range(2)]
    stage(score=(first_diag + 1, buf_a, tk, late[0]))
    stage(score=(first_diag + 1, buf_a, tk, late[1]), value=(first_diag + 1, buf_a, tk, late[0]))
    stage(score=(first_diag, buf_b, 0, head[0]), value=(first_diag + 1, buf_a, tk, late[1]))
    stage(score=(first_diag, buf_b, 0, head[1]), value=(first_diag, buf_b, 0, head[0]))

    def pair(p, prev):
        stage(score=(2 * p, buf_a, None, head[0]), value=(prev, buf_b, None, head[1]))
        stage(score=(2 * p, buf_a, None, head[1]), value=(2 * p, buf_a, None, head[0]))
        stage(score=(2 * p + 1, buf_b, None, head[0]), value=(2 * p, buf_a, None, head[1]))
        stage(score=(2 * p + 1, buf_b, None, head[1]), value=(2 * p + 1, buf_b, None, head[0]))
        return 2 * p + 1

    tiles_needed = None
    for h in range(2):
        gap = None
        for mm in range(2):
            idx = 2 * h + mm
            m_now = jnp.maximum(m_s[idx:idx + 1, :], tmb_s[h:h + 1, mm * tq:(mm + 1) * tq])
            g = jnp.max(score_bound[idx] * BOUND_SLACK_REL + BOUND_SLACK_ABS - m_now, axis=1, keepdims=True)
            gap = g if gap is None else jnp.maximum(gap, g)
        need_h = (gap + ZERO_PROB_EXPONENT) / slope_ref[0, h:h + 1, 0:1] + (tk - 1) / tk
        tiles_needed = need_h if tiles_needed is None else jnp.maximum(tiles_needed, need_h)
    n_full = jnp.full((1, 1), first_diag, jnp.int32)
    first_tile = jnp.ceil(n_full.astype(F32) - tiles_needed - 0.01).astype(jnp.int32)
    first_pair = jnp.max(jnp.clip(first_tile, 0, n_full)) // 2

    last = lax.fori_loop(first_pair, i, pair, first_diag)
    stage(value=(last, buf_b, None, head[1]))

    lamv = lam_ref[...]
    lam = (jnp.exp(jnp.sum(lamv[0:1] * lamv[1:2], axis=1, keepdims=True))
           - jnp.exp(jnp.sum(lamv[2:3] * lamv[3:4], axis=1, keepdims=True)) + LAM_INIT)
    gain = jnp.concatenate([g_ref[...]] * (tq // LANES), axis=1)
    ys = []
    for h in range(2):
        a0 = acc_s[2 * h]
        a1 = acc_s[2 * h + 1]
        a = a0[0:DIFF_V] / a0[DIFF_V:DIFF_V + 1] - lam * (a1[0:DIFF_V] / a1[DIFF_V:DIFF_V + 1])
        ms = jnp.mean(a * a, axis=0, keepdims=True)
        ys.append(a * lax.rsqrt(ms + RMS_EPS) * gain * (1.0 - LAM_INIT))
    o_ref[0] = (jnp.concatenate(ys, axis=0).T * sz_ref[0].astype(F32)).astype(BF16)


def _attn(q, k, vt, kabs, sz, bias, slope_tk, lam_vecs, gain):
    b, s, _ = q.shape
    tq, tk = Q_TILE, K_TILE
    pairs = DIFF_HEADS // 2
    return pl.pallas_call(
        _attn_kernel,
        out_shape=jax.ShapeDtypeStruct((b, s, BRANCH), BF16),
        grid=(b, pairs, s // tq),
        in_specs=[pl.BlockSpec((1, tq, LANES), lambda bi, hp, qi: (bi, qi, hp)),
                  pl.BlockSpec((1, s, LANES), lambda bi, hp, qi: (bi, 0, hp)),
                  pl.BlockSpec((1, s // tk, LANES, tk), lambda bi, hp, qi: (bi, 0, hp, 0)),
                  pl.BlockSpec((1, s // tk, 8, LANES), lambda bi, hp, qi: (bi, 0, 0, hp)),
                  pl.BlockSpec((1, tq, LANES), lambda bi, hp, qi: (bi, qi, hp)),
                  pl.BlockSpec((1, 2, tk, LANES), lambda bi, hp, qi: (hp, 0, 0, 0)),
                  pl.BlockSpec((1, 8, LANES), lambda bi, hp, qi: (hp, 0, 0)),
                  _const_spec((8, LANES), single_buffer=False),
                  _const_spec((DIFF_V, LANES), single_buffer=False)],
        out_specs=pl.BlockSpec((1, tq, LANES), lambda bi, hp, qi: (bi, qi, hp)),
        scratch_shapes=[pltpu.VMEM((2, LANES, 2 * tq), BF16),
                        pltpu.VMEM((8, tq), F32),
                        pltpu.VMEM((4, V_AUG, tq), F32),
                        pltpu.VMEM((2, tk, 2 * tq), F32),
                        pltpu.VMEM((2, tk, 2 * tq), F32),
                        pltpu.VMEM((8, 2 * tq), F32),
                        pltpu.VMEM((8, 2 * tq), F32)],
        compiler_params=pltpu.CompilerParams(dimension_semantics=("parallel", "parallel", "arbitrary"),
                                             vmem_limit_bytes=VMEM_LIMIT),
        name="attn",
    )(q, k, vt, kabs, sz, bias, slope_tk, lam_vecs, gain)


def _epi_kernel(x_ref, odiff_ref, part_ref, gd_ref, lng_ref, lnb_ref, wb1_ref, wout_ref, og_ref, ob_ref, o_ref):
    h = _layer_norm(x_ref[0], lng_ref[...], lnb_ref[...])
    y_diff = jnp.dot(odiff_ref[0], wb1_ref[...], preferred_element_type=F32)
    merged = part_ref[0].astype(F32) + gd_ref[0].astype(F32) * y_diff
    out = jnp.dot(merged.astype(BF16), wout_ref[...], preferred_element_type=F32)
    o_ref[0] = _layer_norm(DEEPNORM_ALPHA * h + out, og_ref[...], ob_ref[...])


def _epi(x, o_diff, part, gd, ln_g, ln_b, wb1, w_out, out_g, out_b):
    b, s, _ = x.shape
    tt = TOKEN_TILE
    tok = lambda width: pl.BlockSpec((1, tt, width), lambda bi, si: (bi, si, 0))
    return pl.pallas_call(
        _epi_kernel,
        out_shape=jax.ShapeDtypeStruct((b, s, D_MODEL), F32),
        grid=(b, s // tt),
        in_specs=[tok(D_MODEL), tok(BRANCH), tok(D_MODEL), tok(D_MODEL),
                  _const_spec((1, D_MODEL)), _const_spec((1, D_MODEL)),
                  _const_spec((BRANCH, D_MODEL)), _const_spec((D_MODEL, D_MODEL)),
                  _const_spec((1, D_MODEL)), _const_spec((1, D_MODEL))],
        out_specs=tok(D_MODEL),
        compiler_params=pltpu.CompilerParams(dimension_semantics=("parallel", "parallel"),
                                             vmem_limit_bytes=VMEM_LIMIT),
        name="epi",
    )(x, o_diff, part, gd, ln_g, ln_b, wb1, w_out, out_g, out_b)


def _alibi_tables():
    pos = np.arange(K_TILE)
    lo = (pos % 256).astype(np.float32)
    hi = (pos - pos % 256).astype(np.float32)
    slopes = [2.0 ** (-8.0 * (h + 1) / DIFF_HEADS) for h in range(DIFF_HEADS)]
    bias = np.zeros((DIFF_HEADS // 2, 2, K_TILE, LANES), np.float32)
    slope_tk = np.zeros((DIFF_HEADS // 2, 8, LANES), np.float32)
    for hp in range(DIFF_HEADS // 2):
        for h in range(2):
            sl = slopes[2 * hp + h]
            for t in range(len(LOG2E_TERMS)):
                bias[hp, h, :, _alibi_lane(h) + 2 * t] = sl * lo
                bias[hp, h, :, _alibi_lane(h) + 2 * t + 1] = sl * hi
            slope_tk[hp, h, :] = sl * K_TILE * LOG2E
    return jnp.asarray(bias, BF16), jnp.asarray(slope_tk, F32)


def kernel(x, mem, ln_in_g, ln_in_b, w_in, b_gate, pool_w, pool_scale, lambda_q1, lambda_k1, lambda_q2, lambda_k2,
           diff_norm_g, w_mem_kv, w_branch, w_out, ln_out_g, ln_out_b):
    assert w_in.shape[0] == DEPTH == 1
    assert x.shape[2] == D_MODEL and x.shape[1] % Q_TILE == 0 and mem.shape[1:] == (MEM_LEN, D_MODEL)
    assert Q_TILE == 2 * K_TILE and K_TILE % CHUNK == 0 and K_TILE <= 512
    w = w_in[0]
    seg = lambda i: w[:, i * BRANCH:(i + 1) * BRANCH]
    w_main = jnp.concatenate([seg(0), seg(1), seg(2), seg(3), seg(5), seg(6), seg(7)], axis=1).astype(BF16)
    w_vt = seg(4).T.astype(BF16)
    w_g = w[:, 8 * BRANCH:].astype(BF16)
    row = lambda v: v.reshape(1, -1).astype(F32)

    mk, mv = _memkv(mem, w_mem_kv[0].astype(BF16))
    q, k, vt, sz, part, gd, kabs = _proj(
        x, row(ln_in_g), row(ln_in_b), w_main, w_vt, w_g, row(b_gate[0]), pool_w[0].astype(BF16),
        row(pool_scale[0]), mk, mv, w_branch[0, 0].astype(BF16), w_branch[0, 2].astype(BF16))

    bias, slope_tk = _alibi_tables()
    lam_vecs = jnp.concatenate([lambda_q1, lambda_k1, lambda_q2, lambda_k2], axis=0).astype(F32)
    lam_vecs = jnp.pad(lam_vecs, ((0, 4), (0, LANES - DIFF_QK)))
    gain = jnp.broadcast_to(diff_norm_g[0].astype(F32)[:, None], (DIFF_V, LANES))
    o_diff = _attn(q, k, vt, kabs, sz, bias, slope_tk, lam_vecs, gain)

    return _epi(x, o_diff, part, gd, row(ln_in_g), row(ln_in_b), w_branch[0, 1].astype(BF16),
                w_out[0].astype(BF16), row(ln_out_g[0]), row(ln_out_b[0]))
```

```python
import math
import struct

import jax
import jax.numpy as jnp
import numpy as np
from jax import lax
from jax.experimental import pallas as pl
from jax.experimental.pallas import tpu as pltpu

F32 = jnp.float32
BF16 = jnp.bfloat16

D_MODEL = 1024
BRANCH = 512
MEM_LEN = 256
POOL_WINDOWS = (2, 4, 8, 16)
POOL_GROUP = 128
DIFF_HEADS = 8
DIFF_QK = 32
DIFF_V = 64
MEM_HEADS = 4
MEM_HEAD_DIM = 128
LN_EPS = 1e-5
RMS_EPS = 1e-5
DEPTH = 1
DEEPNORM_ALPHA = (2.0 * DEPTH) ** 0.25
LAM_INIT = 0.8 - 0.6 * math.exp(-0.3 * 0)
LOG2E = math.log2(math.e)
DIFF_SCALE = DIFF_QK ** -0.5 * LOG2E


def _bf16_terms(x, n):
    terms = []
    for _ in range(n):
        bits = struct.unpack("<I", struct.pack("<f", x))[0]
        bits = (bits + 0x7FFF + ((bits >> 16) & 1)) & 0xFFFF0000
        t = struct.unpack("<f", struct.pack("<I", bits))[0]
        terms.append(t)
        x -= t
    return terms


LOG2E_TERMS = _bf16_terms(LOG2E, 3)
MEM_SCALE = MEM_HEAD_DIM ** -0.5
NEG_BIG = -1e30
ZERO_PROB_EXPONENT = 151.0
BOUND_SLACK_REL = 1.001
BOUND_SLACK_ABS = 1.0
NORM_SLACK_REL = 1.01

LANES = 128
BF16_SUBLANES = 16
VMEM_BYTES_V7X = 64 * 1024 * 1024
VMEM_LIMIT = VMEM_BYTES_V7X - 8 * 1024 * 1024

TOKEN_TILE = 512
K_TILE = TOKEN_TILE
Q_TILE = 2 * K_TILE
CHUNK = 256
HALO = 32
V_AUG = DIFF_V + BF16_SUBLANES
ALIBI_COLS = 2 * len(LOG2E_TERMS)
NT_DIMS = (((1,), (1,)), ((), ()))


def _alibi_lane(h):
    return 2 * DIFF_QK * (1 - h)


def _layer_norm(x, g, b):
    mu = jnp.mean(x, axis=-1, keepdims=True)
    xc = x - mu
    var = jnp.mean(xc * xc, axis=-1, keepdims=True)
    return xc * lax.rsqrt(var + LN_EPS) * g + b


def _silu(x):
    return x * jax.nn.sigmoid(x)


def _const_spec(shape, single_buffer=True):
    n = len(shape)
    mode = pl.Buffered(1) if single_buffer else None
    return pl.BlockSpec(shape, lambda *_: (0,) * n, pipeline_mode=mode)


def _memkv_kernel(mem_ref, w_ref, mk_ref, mv_ref):
    kv = jnp.dot(mem_ref[0].astype(BF16), w_ref[...], preferred_element_type=F32)
    mk_ref[0] = kv[:, :BRANCH].astype(BF16)
    mv_ref[0] = kv[:, BRANCH:].astype(BF16)


def _memkv(mem, w_kv):
    b = mem.shape[0]
    return pl.pallas_call(
        _memkv_kernel,
        out_shape=(jax.ShapeDtypeStruct((b, MEM_LEN, BRANCH), BF16),
                   jax.ShapeDtypeStruct((b, MEM_LEN, BRANCH), BF16)),
        grid=(b,),
        in_specs=[pl.BlockSpec((1, MEM_LEN, D_MODEL), lambda i: (i, 0, 0)),
                  _const_spec((D_MODEL, 2 * BRANCH))],
        out_specs=(pl.BlockSpec((1, MEM_LEN, BRANCH), lambda i: (i, 0, 0)),
                   pl.BlockSpec((1, MEM_LEN, BRANCH), lambda i: (i, 0, 0))),
        name="memkv",
    )(mem, w_kv)


def _proj_kernel(x_ref, lng_ref, lnb_ref, wmain_ref, wvt_ref, wg_ref, bg_ref, poolw_ref, pscale_ref,
                 mk_ref, mv_ref, wb0_ref, wb2_ref, segsum_ref,
                 q_ref, k_ref, vt_ref, sz_ref, part_ref, gd_ref, kabs_ref,
                 e1_ref, e2_ref, e4_ref, e8_ref):
    tt = TOKEN_TILE
    s_idx = pl.program_id(1)
    h = _layer_norm(x_ref[0], lng_ref[...], lnb_ref[...])
    hb = h.astype(BF16)

    def proj(seg):
        return jnp.dot(hb, wmain_ref[:, seg * BRANCH:(seg + 1) * BRANCH], preferred_element_type=F32)

    u = proj(0)

    @pl.when(s_idx == 0)
    def _():
        e1_ref[0:HALO, :] = jnp.zeros((HALO, BRANCH), F32)

    e1_ref[HALO:HALO + tt, :] = u
    n = HALO + tt
    e2_ref[8:n, :] = e1_ref[8:n, :] + e1_ref[7:n - 1, :]
    e4_ref[16:n, :] = e2_ref[16:n, POOL_GROUP:] + e2_ref[14:n - 2, POOL_GROUP:]
    e8_ref[24:n, :] = e4_ref[24:n, POOL_GROUP:] + e4_ref[20:n - 4, POOL_GROUP:]
    s16 = e8_ref[HALO:n, POOL_GROUP:] + e8_ref[HALO - 8:n - 8, POOL_GROUP:]
    wsum = (e2_ref[HALO:n, 0:POOL_GROUP], e4_ref[HALO:n, 0:POOL_GROUP], e8_ref[HALO:n, 0:POOL_GROUP], s16)
    e1_ref[0:HALO, :] = e1_ref[tt:tt + HALO, :]

    t_pos = s_idx * tt + lax.broadcasted_iota(jnp.int32, (tt, POOL_GROUP), 0)
    pooled = []
    for g, w in enumerate(POOL_WINDOWS):
        cnt = jnp.minimum(t_pos + 1, w).astype(F32)
        pg = wsum[g] / cnt - u[:, g * POOL_GROUP:(g + 1) * POOL_GROUP]
        pooled.append(jnp.dot(pg.astype(BF16), poolw_ref[g], preferred_element_type=F32))
    o_pool = jnp.concatenate(pooled, axis=1) * pscale_ref[...] * _silu(proj(1))

    mq = (proj(5) * MEM_SCALE).astype(BF16)
    heads = []
    for hh in range(MEM_HEADS):
        sl = slice(hh * MEM_HEAD_DIM, (hh + 1) * MEM_HEAD_DIM)
        s = lax.dot_general(mq[:, sl], mk_ref[0, :, sl], NT_DIMS, preferred_element_type=F32)
        p = jnp.exp(s - jnp.max(s, axis=-1, keepdims=True))
        l = jnp.sum(p, axis=-1, keepdims=True)
        heads.append(jnp.dot(p.astype(BF16), mv_ref[0, :, sl], preferred_element_type=F32) / l)
    o_mem = jnp.concatenate(heads, axis=1) * _silu(proj(6))

    def gate(nb):
        sl = slice(nb * D_MODEL, (nb + 1) * D_MODEL)
        return jax.nn.sigmoid(jnp.dot(hb, wg_ref[:, sl], preferred_element_type=F32) + bg_ref[:, sl])

    y_pool = jnp.dot(o_pool.astype(BF16), wb0_ref[...], preferred_element_type=F32)
    y_mem = jnp.dot(o_mem.astype(BF16), wb2_ref[...], preferred_element_type=F32)
    part_ref[0] = (gate(0) * y_pool + gate(2) * y_mem).astype(BF16)
    gd_ref[0] = gate(1).astype(BF16)

    q_ref[0] = (proj(2) * DIFF_SCALE).astype(BF16)
    kb = proj(3).astype(BF16)
    k_ref[0] = kb
    kf = kb.astype(F32)
    norm2 = jnp.dot((kf * kf).astype(BF16), segsum_ref[...], preferred_element_type=F32)
    kabs_ref[0, 0] = jnp.concatenate(
        [jnp.max(jnp.abs(kf), axis=0, keepdims=True),
         jnp.concatenate([jnp.max(norm2, axis=0, keepdims=True)] * (BRANCH // LANES), axis=1),
         jnp.zeros((6, BRANCH), F32)], axis=0)
    vt = lax.dot_general(wvt_ref[...], hb, NT_DIMS, preferred_element_type=F32)
    vt_ref[0, 0] = vt.astype(BF16)
    sz_ref[0] = _silu(proj(4)).astype(BF16)


def _proj(x, ln_g, ln_b, w_main, w_vt, w_g, b_g, pool_w, pool_scale, mk, mv, wb0, wb2):
    b, s, _ = x.shape
    tt = TOKEN_TILE
    ns = s // tt
    tok = lambda width: pl.BlockSpec((1, tt, width), lambda bi, si: (bi, si, 0))
    per_batch = pl.BlockSpec((1, MEM_LEN, BRANCH), lambda bi, si: (bi, 0, 0))
    segsum = jnp.asarray(np.arange(BRANCH)[:, None] // DIFF_QK == np.arange(LANES)[None, :], BF16)
    return pl.pallas_call(
        _proj_kernel,
        out_shape=(jax.ShapeDtypeStruct((b, s, BRANCH), BF16),
                   jax.ShapeDtypeStruct((b, s, BRANCH), BF16),
                   jax.ShapeDtypeStruct((b, ns, BRANCH, tt), BF16),
                   jax.ShapeDtypeStruct((b, s, BRANCH), BF16),
                   jax.ShapeDtypeStruct((b, s, D_MODEL), BF16),
                   jax.ShapeDtypeStruct((b, s, D_MODEL), BF16),
                   jax.ShapeDtypeStruct((b, ns, 8, BRANCH), F32)),
        grid=(b, ns),
        in_specs=[tok(D_MODEL),
                  _const_spec((1, D_MODEL)), _const_spec((1, D_MODEL)),
                  _const_spec((D_MODEL, 7 * BRANCH)), _const_spec((BRANCH, D_MODEL)),
                  _const_spec((D_MODEL, 3 * D_MODEL)), _const_spec((1, 3 * D_MODEL)),
                  _const_spec((len(POOL_WINDOWS), POOL_GROUP, POOL_GROUP)), _const_spec((1, BRANCH)),
                  per_batch, per_batch,
                  _const_spec((BRANCH, D_MODEL)), _const_spec((BRANCH, D_MODEL)),
                  _const_spec((BRANCH, LANES))],
        out_specs=(tok(BRANCH), tok(BRANCH),
                   pl.BlockSpec((1, 1, BRANCH, tt), lambda bi, si: (bi, si, 0, 0)),
                   tok(BRANCH), tok(D_MODEL), tok(D_MODEL),
                   pl.BlockSpec((1, 1, 8, BRANCH), lambda bi, si: (bi, si, 0, 0))),
        scratch_shapes=[pltpu.VMEM((HALO + tt, BRANCH), F32),
                        pltpu.VMEM((HALO + tt, BRANCH), F32),
                        pltpu.VMEM((HALO + tt, BRANCH - POOL_GROUP), F32),
                        pltpu.VMEM((HALO + tt, BRANCH - 2 * POOL_GROUP), F32)],
        compiler_params=pltpu.CompilerParams(dimension_semantics=("arbitrary", "arbitrary"),
                                             vmem_limit_bytes=VMEM_LIMIT),
        name="proj",
    )(x, ln_g, ln_b, w_main, w_vt, w_g, b_g, pool_w, pool_scale, mk, mv, wb0, wb2, segsum)


def _attn_kernel(q_ref, k_ref, vt_ref, kabs_ref, sz_ref, bias_ref, slope_ref, lam_ref, g_ref, o_ref,
                 qt_s, m_s, acc_s,
                 sa_s, sb_s, tma_s, tmb_s):
    tq, tk = Q_TILE, K_TILE
    i = pl.program_id(2)

    qt32 = q_ref[0].astype(F32).T
    qt = qt32.astype(BF16)
    kstat = jnp.max(kabs_ref[0], axis=0)
    kcol = kstat.T[:, 0:1]
    qk_bound = jnp.abs(qt32) * kcol
    q_sq = qt32 * qt32
    lane = lax.broadcasted_iota(jnp.int32, (1, LANES), 1)
    score_bound = []
    for idx in range(4):
        rows = slice(DIFF_QK * idx, DIFF_QK * (idx + 1))
        k_sq = jnp.max(jnp.where(lane == 4 * pl.program_id(1) + idx, kstat[1:2], 0.0), axis=1, keepdims=True)
        by_norm = jnp.sqrt(jnp.sum(q_sq[rows], axis=0, keepdims=True) * (k_sq * NORM_SLACK_REL))
        score_bound.append(jnp.minimum(jnp.sum(qk_bound[rows], axis=0, keepdims=True), by_norm))

    @pl.when(i == 0)
    def _():
        row = lax.broadcasted_iota(jnp.int32, (BF16_SUBLANES, 2 * tq), 0)
        for h in range(2):
            qt_s[h] = jnp.zeros((LANES, 2 * tq), BF16)
            alibi_rows = jnp.zeros((BF16_SUBLANES, 2 * tq), F32)
            for t, c in enumerate(LOG2E_TERMS):
                alibi_rows = jnp.where((row == 2 * t) | (row == 2 * t + 1), c, alibi_rows)
            qt_s[h, _alibi_lane(h):_alibi_lane(h) + BF16_SUBLANES, :] = alibi_rows.astype(BF16)

    for h in range(2):
        base = 2 * DIFF_QK * h
        qt_s[h, base:base + DIFF_QK, 0:tq] = qt[base:base + DIFF_QK]
        qt_s[h, base + DIFF_QK:base + 2 * DIFF_QK, tq:2 * tq] = qt[base + DIFF_QK:base + 2 * DIFF_QK]
    m_s[...] = jnp.full(m_s.shape, NEG_BIG, F32)
    acc_s[...] = jnp.zeros(acc_s.shape, F32)

    ones_rows = jnp.where(lax.broadcasted_iota(jnp.int32, (BF16_SUBLANES, tk), 0) == 0, 1.0, 0.0).astype(BF16)
    slope_tk = [jnp.concatenate([slope_ref[0, h:h + 1, :]] * (tq // LANES), axis=1) for h in range(2)]

    cw = CHUNK
    chunks = [(h, n) for h in range(2) for n in range(2 * tq // cw)]

    def q_start(n):
        return (n * cw) % tq

    def live_rows(n, key_start):
        return tk if key_start is None else min(tk, q_start(n) + cw - key_start)

    def score_chunk(lhs, s_buf, tm_buf, h, n, key_start):
        cols = slice(n * cw, (n + 1) * cw)
        kr = live_rows(n, key_start)
        st = jnp.dot(lhs[h][:kr], qt_s[h, :, cols], preferred_element_type=F32)
        if key_start is not None and key_start + kr - 1 > q_start(n):
            kk = lax.broadcasted_iota(jnp.int32, (kr, cw), 0) + key_start
            qq = lax.broadcasted_iota(jnp.int32, (kr, cw), 1) + q_start(n)
            st = jnp.where(kk <= qq, st, NEG_BIG)
        s_buf[h, :kr, cols] = st
        if kr < tk:
            s_buf[h, kr:, cols] = jnp.full((tk - kr, cw), NEG_BIG, F32)
        tm_buf[h:h + 1, cols] = jnp.max(st, axis=0, keepdims=True)

    def value_chunk(vblk, dj, s_buf, tm_buf, h, n, key_start):
        idx = 2 * h + (n * cw) // tq
        cols = slice(n * cw, (n + 1) * cw)
        qcols = slice(q_start(n), q_start(n) + cw)
        kr = live_rows(n, key_start)
        off = slope_tk[h][:, qcols] * dj
        vaug = jnp.concatenate([vblk[DIFF_V * h:DIFF_V * (h + 1)], ones_rows], axis=0)
        m_old = m_s[idx:idx + 1, qcols]
        m_new = jnp.maximum(m_old, tm_buf[h:h + 1, cols] + off)
        alpha = jnp.exp2(m_old - m_new)
        p = jnp.exp2(s_buf[h, :kr, cols] - (m_new - off)).astype(BF16)
        pv = jnp.dot(vaug[:, :kr], p, preferred_element_type=F32)
        acc_s[idx, :, qcols] = alpha * acc_s[idx, :, qcols] + pv
        m_s[idx:idx + 1, qcols] = m_new

    def key_lhs(blk):
        kblk = k_ref[0, pl.ds(pl.multiple_of(blk * tk, tk), tk), :]
        own = lax.broadcasted_iota(jnp.int32, (tk, LANES), 1) < 2 * DIFF_QK
        return [jnp.where(own, kblk, bias_ref[0, 0]), jnp.where(own, bias_ref[0, 1], kblk)]

    def stage(score=None, value=None):
        s_chunks, v_chunks = [], []
        if score is not None:
            s_blk, s_bufs, s_start, s_chunks = score
            lhs = key_lhs(s_blk)
        if value is not None:
            v_blk, v_bufs, v_start, v_chunks = value
            vblk = vt_ref[0, v_blk]
            dj = jnp.full((1, cw), v_blk - first_diag, jnp.int32).astype(F32)
        for c in range(max(len(s_chunks), len(v_chunks))):
            if c < len(s_chunks):
                score_chunk(lhs, *s_bufs, *s_chunks[c], s_start)
            if c < len(v_chunks):
                value_chunk(vblk, dj, *v_bufs, *v_chunks[c], v_start)

    buf_a, buf_b = (sa_s, tma_s), (sb_s, tmb_s)
    first_diag = (tq // tk) * i
    head = [[c for c in chunks if c[0] == h] for h in range(2)]
    late = [[c for c in head[h] if q_start(c[1]) >= tk] for h in range(2)]
    stage(score=(first_diag + 1, buf_a, tk, late[0]))
    stage(score=(first_diag + 1, buf_a, tk, late[1]), value=(first_diag + 1, buf_a, tk, late[0]))
    stage(score=(first_diag, buf_b, 0, head[0]), value=(first_diag + 1, buf_a, tk, late[1]))
    stage(score=(first_diag, buf_b, 0, head[1]), value=(first_diag, buf_b, 0, head[0]))

    def pair(p, prev):
        stage(score=(2 * p, buf_a, None, head[0]), value=(prev, buf_b, None, head[1]))
        stage(score=(2 * p, buf_a, None, head[1]), value=(2 * p, buf_a, None, head[0]))
        stage(score=(2 * p + 1, buf_b, None, head[0]), value=(2 * p, buf_a, None, head[1]))
        stage(score=(2 * p + 1, buf_b, None, head[1]), value=(2 * p + 1, buf_b, None, head[0]))
        return 2 * p + 1

    tiles_needed = None
    for h in range(2):
        gap = None
        for mm in range(2):
            idx = 2 * h + mm
            m_now = jnp.maximum(m_s[idx:idx + 1, :], tmb_s[h:h + 1, mm * tq:(mm + 1) * tq])
            g = jnp.max(score_bound[idx] * BOUND_SLACK_REL + BOUND_SLACK_ABS - m_now, axis=1, keepdims=True)
            gap = g if gap is None else jnp.maximum(gap, g)
        need_h = (gap + ZERO_PROB_EXPONENT) / slope_ref[0, h:h + 1, 0:1] + (tk - 1) / tk
        tiles_needed = need_h if tiles_needed is None else jnp.maximum(tiles_needed, need_h)
    n_full = jnp.full((1, 1), first_diag, jnp.int32)
    first_tile = jnp.ceil(n_full.astype(F32) - tiles_needed - 0.01).astype(jnp.int32)
    first_tile = jnp.max(jnp.clip(first_tile, 0, n_full))
    starts_odd = first_tile % 2

    @pl.when(starts_odd == 1)
    def _():
        stage(score=(first_tile, buf_b, None, head[0]), value=(first_diag, buf_b, None, head[1]))
        stage(score=(first_tile, buf_b, None, head[1]), value=(first_tile, buf_b, None, head[0]))

    pending = jnp.where(starts_odd == 1, first_tile, first_diag)
    last = lax.fori_loop(first_tile // 2 + starts_odd, i, pair, pending)
    stage(value=(last, buf_b, None, head[1]))

    lamv = lam_ref[...]
    lam = (jnp.exp(jnp.sum(lamv[0:1] * lamv[1:2], axis=1, keepdims=True))
           - jnp.exp(jnp.sum(lamv[2:3] * lamv[3:4], axis=1, keepdims=True)) + LAM_INIT)
    gain = jnp.concatenate([g_ref[...]] * (tq // LANES), axis=1)
    ys = []
    for h in range(2):
        a0 = acc_s[2 * h]
        a1 = acc_s[2 * h + 1]
        a = a0[0:DIFF_V] / a0[DIFF_V:DIFF_V + 1] - lam * (a1[0:DIFF_V] / a1[DIFF_V:DIFF_V + 1])
        ms = jnp.mean(a * a, axis=0, keepdims=True)
        ys.append(a * lax.rsqrt(ms + RMS_EPS) * gain * (1.0 - LAM_INIT))
    o_ref[0] = (jnp.concatenate(ys, axis=0).T * sz_ref[0].astype(F32)).astype(BF16)


def _attn(q, k, vt, kabs, sz, bias, slope_tk, lam_vecs, gain):
    b, s, _ = q.shape
    tq, tk = Q_TILE, K_TILE
    pairs = DIFF_HEADS // 2
    return pl.pallas_call(
        _attn_kernel,
        out_shape=jax.ShapeDtypeStruct((b, s, BRANCH), BF16),
        grid=(b, pairs, s // tq),
        in_specs=[pl.BlockSpec((1, tq, LANES), lambda bi, hp, qi: (bi, qi, hp)),
                  pl.BlockSpec((1, s, LANES), lambda bi, hp, qi: (bi, 0, hp)),
                  pl.BlockSpec((1, s // tk, LANES, tk), lambda bi, hp, qi: (bi, 0, hp, 0)),
                  pl.BlockSpec((1, s // tk, 8, LANES), lambda bi, hp, qi: (bi, 0, 0, hp)),
                  pl.BlockSpec((1, tq, LANES), lambda bi, hp, qi: (bi, qi, hp)),
                  pl.BlockSpec((1, 2, tk, LANES), lambda bi, hp, qi: (hp, 0, 0, 0)),
                  pl.BlockSpec((1, 8, LANES), lambda bi, hp, qi: (hp, 0, 0)),
                  _const_spec((8, LANES), single_buffer=False),
                  _const_spec((DIFF_V, LANES), single_buffer=False)],
        out_specs=pl.BlockSpec((1, tq, LANES), lambda bi, hp, qi: (bi, qi, hp)),
        scratch_shapes=[pltpu.VMEM((2, LANES, 2 * tq), BF16),
                        pltpu.VMEM((8, tq), F32),
                        pltpu.VMEM((4, V_AUG, tq), F32),
                        pltpu.VMEM((2, tk, 2 * tq), F32),
                        pltpu.VMEM((2, tk, 2 * tq), F32),
                        pltpu.VMEM((8, 2 * tq), F32),
                        pltpu.VMEM((8, 2 * tq), F32)],
        compiler_params=pltpu.CompilerParams(dimension_semantics=("parallel", "parallel", "arbitrary"),
                                             vmem_limit_bytes=VMEM_LIMIT),
        name="attn",
    )(q, k, vt, kabs, sz, bias, slope_tk, lam_vecs, gain)


def _epi_kernel(x_ref, odiff_ref, part_ref, gd_ref, lng_ref, lnb_ref, wb1_ref, wout_ref, og_ref, ob_ref, o_ref):
    h = _layer_norm(x_ref[0], lng_ref[...], lnb_ref[...])
    y_diff = jnp.dot(odiff_ref[0], wb1_ref[...], preferred_element_type=F32)
    merged = part_ref[0].astype(F32) + gd_ref[0].astype(F32) * y_diff
    out = jnp.dot(merged.astype(BF16), wout_ref[...], preferred_element_type=F32)
    o_ref[0] = _layer_norm(DEEPNORM_ALPHA * h + out, og_ref[...], ob_ref[...])


def _epi(x, o_diff, part, gd, ln_g, ln_b, wb1, w_out, out_g, out_b):
    b, s, _ = x.shape
    tt = TOKEN_TILE
    tok = lambda width: pl.BlockSpec((1, tt, width), lambda bi, si: (bi, si, 0))
    return pl.pallas_call(
        _epi_kernel,
        out_shape=jax.ShapeDtypeStruct((b, s, D_MODEL), F32),
        grid=(b, s // tt),
        in_specs=[tok(D_MODEL), tok(BRANCH), tok(D_MODEL), tok(D_MODEL),
                  _const_spec((1, D_MODEL)), _const_spec((1, D_MODEL)),
                  _const_spec((BRANCH, D_MODEL)), _const_spec((D_MODEL, D_MODEL)),
                  _const_spec((1, D_MODEL)), _const_spec((1, D_MODEL))],
        out_specs=tok(D_MODEL),
        compiler_params=pltpu.CompilerParams(dimension_semantics=("parallel", "parallel"),
                                             vmem_limit_bytes=VMEM_LIMIT),
        name="epi",
    )(x, o_diff, part, gd, ln_g, ln_b, wb1, w_out, out_g, out_b)


def _alibi_tables():
    pos = np.arange(K_TILE)
    lo = (pos % 256).astype(np.float32)
    hi = (pos - pos % 256).astype(np.float32)
    slopes = [2.0 ** (-8.0 * (h + 1) / DIFF_HEADS) for h in range(DIFF_HEADS)]
    bias = np.zeros((DIFF_HEADS // 2, 2, K_TILE, LANES), np.float32)
    slope_tk = np.zeros((DIFF_HEADS // 2, 8, LANES), np.float32)
    for hp in range(DIFF_HEADS // 2):
        for h in range(2):
            sl = slopes[2 * hp + h]
            for t in range(len(LOG2E_TERMS)):
                bias[hp, h, :, _alibi_lane(h) + 2 * t] = sl * lo
                bias[hp, h, :, _alibi_lane(h) + 2 * t + 1] = sl * hi
            slope_tk[hp, h, :] = sl * K_TILE * LOG2E
    return jnp.asarray(bias, BF16), jnp.asarray(slope_tk, F32)


def kernel(x, mem, ln_in_g, ln_in_b, w_in, b_gate, pool_w, pool_scale, lambda_q1, lambda_k1, lambda_q2, lambda_k2,
           diff_norm_g, w_mem_kv, w_branch, w_out, ln_out_g, ln_out_b):
    assert w_in.shape[0] == DEPTH == 1
    assert x.shape[2] == D_MODEL and x.shape[1] % Q_TILE == 0 and mem.shape[1:] == (MEM_LEN, D_MODEL)
    assert Q_TILE == 2 * K_TILE and K_TILE % CHUNK == 0 and K_TILE <= 512
    w = w_in[0]
    seg = lambda i: w[:, i * BRANCH:(i + 1) * BRANCH]
    w_main = jnp.concatenate([seg(0), seg(1), seg(2), seg(3), seg(5), seg(6), seg(7)], axis=1).astype(BF16)
    w_vt = seg(4).T.astype(BF16)
    w_g = w[:, 8 * BRANCH:].astype(BF16)
    row = lambda v: v.reshape(1, -1).astype(F32)

    mk, mv = _memkv(mem, w_mem_kv[0].astype(BF16))
    q, k, vt, sz, part, gd, kabs = _proj(
        x, row(ln_in_g), row(ln_in_b), w_main, w_vt, w_g, row(b_gate[0]), pool_w[0].astype(BF16),
        row(pool_scale[0]), mk, mv, w_branch[0, 0].astype(BF16), w_branch[0, 2].astype(BF16))

    bias, slope_tk = _alibi_tables()
    lam_vecs = jnp.concatenate([lambda_q1, lambda_k1, lambda_q2, lambda_k2], axis=0).astype(F32)
    lam_vecs = jnp.pad(lam_vecs, ((0, 4), (0, LANES - DIFF_QK)))
    gain = jnp.broadcast_to(diff_norm_g[0].astype(F32)[:, None], (DIFF_V, LANES))
    o_diff = _attn(q, k, vt, kabs, sz, bias, slope_tk, lam_vecs, gain)

    return _epi(x, o_diff, part, gd, row(ln_in_g), row(ln_in_b), w_branch[0, 1].astype(BF16),
                w_out[0].astype(BF16), row(ln_out_g[0]), row(ln_out_b[0]))
```

```python
import math
import struct

import jax
import jax.numpy as jnp
import numpy as np
from jax import lax
from jax.experimental import pallas as pl
from jax.experimental.pallas import tpu as pltpu

F32 = jnp.float32
BF16 = jnp.bfloat16

D_MODEL = 1024
BRANCH = 512
MEM_LEN = 256
POOL_WINDOWS = (2, 4, 8, 16)
POOL_GROUP = 128
DIFF_HEADS = 8
DIFF_QK = 32
DIFF_V = 64
MEM_HEADS = 4
MEM_HEAD_DIM = 128
LN_EPS = 1e-5
RMS_EPS = 1e-5
DEPTH = 1
DEEPNORM_ALPHA = (2.0 * DEPTH) ** 0.25
LAM_INIT = 0.8 - 0.6 * math.exp(-0.3 * 0)
LOG2E = math.log2(math.e)
DIFF_SCALE = DIFF_QK ** -0.5 * LOG2E


def _bf16_terms(x, n):
    terms = []
    for _ in range(n):
        bits = struct.unpack("<I", struct.pack("<f", x))[0]
        bits = (bits + 0x7FFF + ((bits >> 16) & 1)) & 0xFFFF0000
        t = struct.unpack("<f", struct.pack("<I", bits))[0]
        terms.append(t)
        x -= t
    return terms


LOG2E_TERMS = _bf16_terms(LOG2E, 3)
MEM_SCALE = MEM_HEAD_DIM ** -0.5
NEG_BIG = -1e30
ZERO_PROB_EXPONENT = 151.0
BOUND_SLACK_REL = 1.001
BOUND_SLACK_ABS = 1.0
NORM_SLACK_REL = 1.01

LANES = 128
BF16_SUBLANES = 16
VMEM_BYTES_V7X = 64 * 1024 * 1024
VMEM_LIMIT = VMEM_BYTES_V7X - 8 * 1024 * 1024

TOKEN_TILE = 512
K_TILE = TOKEN_TILE
Q_TILE = 2 * K_TILE
CHUNK = 256
HALO = 32
V_AUG = DIFF_V + BF16_SUBLANES
ALIBI_COLS = 2 * len(LOG2E_TERMS)
NT_DIMS = (((1,), (1,)), ((), ()))


def _alibi_lane(h):
    return 2 * DIFF_QK * (1 - h)


def _layer_norm(x, g, b):
    mu = jnp.mean(x, axis=-1, keepdims=True)
    xc = x - mu
    var = jnp.mean(xc * xc, axis=-1, keepdims=True)
    return xc * lax.rsqrt(var + LN_EPS) * g + b


def _silu(x):
    return x * jax.nn.sigmoid(x)


def _const_spec(shape, single_buffer=True):
    n = len(shape)
    mode = pl.Buffered(1) if single_buffer else None
    return pl.BlockSpec(shape, lambda *_: (0,) * n, pipeline_mode=mode)


def _memkv_kernel(mem_ref, w_ref, mk_ref, mv_ref):
    kv = jnp.dot(mem_ref[0].astype(BF16), w_ref[...], preferred_element_type=F32)
    mk_ref[0] = kv[:, :BRANCH].astype(BF16)
    mv_ref[0] = kv[:, BRANCH:].astype(BF16)


def _memkv(mem, w_kv):
    b = mem.shape[0]
    return pl.pallas_call(
        _memkv_kernel,
        out_shape=(jax.ShapeDtypeStruct((b, MEM_LEN, BRANCH), BF16),
                   jax.ShapeDtypeStruct((b, MEM_LEN, BRANCH), BF16)),
        grid=(b,),
        in_specs=[pl.BlockSpec((1, MEM_LEN, D_MODEL), lambda i: (i, 0, 0)),
                  _const_spec((D_MODEL, 2 * BRANCH))],
        out_specs=(pl.BlockSpec((1, MEM_LEN, BRANCH), lambda i: (i, 0, 0)),
                   pl.BlockSpec((1, MEM_LEN, BRANCH), lambda i: (i, 0, 0))),
        name="memkv",
    )(mem, w_kv)


def _proj_kernel(x_ref, lng_ref, lnb_ref, wmain_ref, wvt_ref, wg_ref, bg_ref, poolw_ref, pscale_ref,
                 mk_ref, mv_ref, wb0_ref, wb2_ref, segsum_ref,
                 q_ref, k_ref, vt_ref, sz_ref, part_ref, gd_ref, kabs_ref,
                 e1_ref, e2_ref, e4_ref, e8_ref):
    tt = TOKEN_TILE
    s_idx = pl.program_id(1)
    h = _layer_norm(x_ref[0], lng_ref[...], lnb_ref[...])
    hb = h.astype(BF16)

    def proj(seg):
        return jnp.dot(hb, wmain_ref[:, seg * BRANCH:(seg + 1) * BRANCH], preferred_element_type=F32)

    u = proj(0)

    @pl.when(s_idx == 0)
    def _():
        e1_ref[0:HALO, :] = jnp.zeros((HALO, BRANCH), F32)

    e1_ref[HALO:HALO + tt, :] = u
    n = HALO + tt
    e2_ref[8:n, :] = e1_ref[8:n, :] + e1_ref[7:n - 1, :]
    e4_ref[16:n, :] = e2_ref[16:n, POOL_GROUP:] + e2_ref[14:n - 2, POOL_GROUP:]
    e8_ref[24:n, :] = e4_ref[24:n, POOL_GROUP:] + e4_ref[20:n - 4, POOL_GROUP:]
    s16 = e8_ref[HALO:n, POOL_GROUP:] + e8_ref[HALO - 8:n - 8, POOL_GROUP:]
    wsum = (e2_ref[HALO:n, 0:POOL_GROUP], e4_ref[HALO:n, 0:POOL_GROUP], e8_ref[HALO:n, 0:POOL_GROUP], s16)
    e1_ref[0:HALO, :] = e1_ref[tt:tt + HALO, :]

    t_pos = s_idx * tt + lax.broadcasted_iota(jnp.int32, (tt, POOL_GROUP), 0)
    pooled = []
    for g, w in enumerate(POOL_WINDOWS):
        cnt = jnp.minimum(t_pos + 1, w).astype(F32)
        pg = wsum[g] / cnt - u[:, g * POOL_GROUP:(g + 1) * POOL_GROUP]
        pooled.append(jnp.dot(pg.astype(BF16), poolw_ref[g], preferred_element_type=F32))
    o_pool = jnp.concatenate(pooled, axis=1) * pscale_ref[...] * _silu(proj(1))

    mq = (proj(5) * MEM_SCALE).astype(BF16)
    heads = []
    for hh in range(MEM_HEADS):
        sl = slice(hh * MEM_HEAD_DIM, (hh + 1) * MEM_HEAD_DIM)
        s = lax.dot_general(mq[:, sl], mk_ref[0, :, sl], NT_DIMS, preferred_element_type=F32)
        p = jnp.exp(s - jnp.max(s, axis=-1, keepdims=True))
        l = jnp.sum(p, axis=-1, keepdims=True)
        heads.append(jnp.dot(p.astype(BF16), mv_ref[0, :, sl], preferred_element_type=F32) / l)
    o_mem = jnp.concatenate(heads, axis=1) * _silu(proj(6))

    def gate(nb):
        sl = slice(nb * D_MODEL, (nb + 1) * D_MODEL)
        return jax.nn.sigmoid(jnp.dot(hb, wg_ref[:, sl], preferred_element_type=F32) + bg_ref[:, sl])

    y_pool = jnp.dot(o_pool.astype(BF16), wb0_ref[...], preferred_element_type=F32)
    y_mem = jnp.dot(o_mem.astype(BF16), wb2_ref[...], preferred_element_type=F32)
    part_ref[0] = (gate(0) * y_pool + gate(2) * y_mem).astype(BF16)
    gd_ref[0] = gate(1).astype(BF16)

    q_ref[0] = (proj(2) * DIFF_SCALE).astype(BF16)
    kb = proj(3).astype(BF16)
    k_ref[0] = kb
    kf = kb.astype(F32)
    norm2 = jnp.dot((kf * kf).astype(BF16), segsum_ref[...], preferred_element_type=F32)
    kabs_ref[0, 0] = jnp.concatenate(
        [jnp.max(jnp.abs(kf), axis=0, keepdims=True),
         jnp.concatenate([jnp.max(norm2, axis=0, keepdims=True)] * (BRANCH // LANES), axis=1),
         jnp.zeros((6, BRANCH), F32)], axis=0)
    vt = lax.dot_general(wvt_ref[...], hb, NT_DIMS, preferred_element_type=F32)
    vt_ref[0, 0] = vt.astype(BF16)
    sz_ref[0] = _silu(proj(4)).astype(BF16)


def _proj(x, ln_g, ln_b, w_main, w_vt, w_g, b_g, pool_w, pool_scale, mk, mv, wb0, wb2):
    b, s, _ = x.shape
    tt = TOKEN_TILE
    ns = s // tt
    tok = lambda width: pl.BlockSpec((1, tt, width), lambda bi, si: (bi, si, 0))
    per_batch = pl.BlockSpec((1, MEM_LEN, BRANCH), lambda bi, si: (bi, 0, 0))
    segsum = jnp.asarray(np.arange(BRANCH)[:, None] // DIFF_QK == np.arange(LANES)[None, :], BF16)
    return pl.pallas_call(
        _proj_kernel,
        out_shape=(jax.ShapeDtypeStruct((b, s, BRANCH), BF16),
                   jax.ShapeDtypeStruct((b, s, BRANCH), BF16),
                   jax.ShapeDtypeStruct((b, ns, BRANCH, tt), BF16),
                   jax.ShapeDtypeStruct((b, s, BRANCH), BF16),
                   jax.ShapeDtypeStruct((b, s, D_MODEL), BF16),
                   jax.ShapeDtypeStruct((b, s, D_MODEL), BF16),
                   jax.ShapeDtypeStruct((b, ns, 8, BRANCH), F32)),
        grid=(b, ns),
        in_specs=[tok(D_MODEL),
                  _const_spec((1, D_MODEL)), _const_spec((1, D_MODEL)),
                  _const_spec((D_MODEL, 7 * BRANCH)), _const_spec((BRANCH, D_MODEL)),
                  _const_spec((D_MODEL, 3 * D_MODEL)), _const_spec((1, 3 * D_MODEL)),
                  _const_spec((len(POOL_WINDOWS), POOL_GROUP, POOL_GROUP)), _const_spec((1, BRANCH)),
                  per_batch, per_batch,
                  _const_spec((BRANCH, D_MODEL)), _const_spec((BRANCH, D_MODEL)),
                  _const_spec((BRANCH, LANES))],
        out_specs=(tok(BRANCH), tok(BRANCH),
                   pl.BlockSpec((1, 1, BRANCH, tt), lambda bi, si: (bi, si, 0, 0)),
                   tok(BRANCH), tok(D_MODEL), tok(D_MODEL),
                   pl.BlockSpec((1, 1, 8, BRANCH), lambda bi, si: (bi, si, 0, 0))),
        scratch_shapes=[pltpu.VMEM((HALO + tt, BRANCH), F32),
                        pltpu.VMEM((HALO + tt, BRANCH), F32),
                        pltpu.VMEM((HALO + tt, BRANCH - POOL_GROUP), F32),
                        pltpu.VMEM((HALO + tt, BRANCH - 2 * POOL_GROUP), F32)],
        compiler_params=pltpu.CompilerParams(dimension_semantics=("arbitrary", "arbitrary"),
                                             vmem_limit_bytes=VMEM_LIMIT),
        name="proj",
    )(x, ln_g, ln_b, w_main, w_vt, w_g, b_g, pool_w, pool_scale, mk, mv, wb0, wb2, segsum)


def _attn_kernel(q_ref, k_ref, vt_ref, kabs_ref, sz_ref, bias_ref, slope_ref, lam_ref, g_ref, o_ref,
                 qt_s, m_s, acc_s,
                 sa_s, sb_s, tma_s, tmb_s):
    tq, tk = Q_TILE, K_TILE
    i = pl.program_id(2)

    qt32 = q_ref[0].astype(F32).T
    qt = qt32.astype(BF16)
    kstat = jnp.max(kabs_ref[0], axis=0)
    kcol = kstat.T[:, 0:1]
    qk_bound = jnp.abs(qt32) * kcol
    q_sq = qt32 * qt32
    lane = lax.broadcasted_iota(jnp.int32, (1, LANES), 1)
    score_bound = []
    for idx in range(4):
        rows = slice(DIFF_QK * idx, DIFF_QK * (idx + 1))
        k_sq = jnp.max(jnp.where(lane == 4 * pl.program_id(1) + idx, kstat[1:2], 0.0), axis=1, keepdims=True)
        by_norm = jnp.sqrt(jnp.sum(q_sq[rows], axis=0, keepdims=True) * (k_sq * NORM_SLACK_REL))
        score_bound.append(jnp.minimum(jnp.sum(qk_bound[rows], axis=0, keepdims=True), by_norm))

    @pl.when(i == 0)
    def _():
        row = lax.broadcasted_iota(jnp.int32, (BF16_SUBLANES, 2 * tq), 0)
        for h in range(2):
            qt_s[h] = jnp.zeros((LANES, 2 * tq), BF16)
            alibi_rows = jnp.zeros((BF16_SUBLANES, 2 * tq), F32)
            for t, c in enumerate(LOG2E_TERMS):
                alibi_rows = jnp.where((row == 2 * t) | (row == 2 * t + 1), c, alibi_rows)
            qt_s[h, _alibi_lane(h):_alibi_lane(h) + BF16_SUBLANES, :] = alibi_rows.astype(BF16)

    for h in range(2):
        base = 2 * DIFF_QK * h
        qt_s[h, base:base + DIFF_QK, 0:tq] = qt[base:base + DIFF_QK]
        qt_s[h, base + DIFF_QK:base + 2 * DIFF_QK, tq:2 * tq] = qt[base + DIFF_QK:base + 2 * DIFF_QK]
    m_s[...] = jnp.full(m_s.shape, NEG_BIG, F32)
    acc_s[...] = jnp.zeros(acc_s.shape, F32)

    ones_rows = jnp.where(lax.broadcasted_iota(jnp.int32, (BF16_SUBLANES, tk), 0) == 0, 1.0, 0.0).astype(BF16)
    slope_tk = [jnp.concatenate([slope_ref[0, h:h + 1, :]] * (tq // LANES), axis=1) for h in range(2)]

    cw = CHUNK
    chunks = [(h, n) for h in range(2) for n in range(2 * tq // cw)]

    def q_start(n):
        return (n * cw) % tq

    def live_rows(n, key_start):
        return tk if key_start is None else min(tk, q_start(n) + cw - key_start)

    def score_chunk(lhs, s_buf, tm_buf, h, n, key_start):
        cols = slice(n * cw, (n + 1) * cw)
        kr = live_rows(n, key_start)
        st = jnp.dot(lhs[h][:kr], qt_s[h, :, cols], preferred_element_type=F32)
        if key_start is not None and key_start + kr - 1 > q_start(n):
            kk = lax.broadcasted_iota(jnp.int32, (kr, cw), 0) + key_start
            qq = lax.broadcasted_iota(jnp.int32, (kr, cw), 1) + q_start(n)
            st = jnp.where(kk <= qq, st, NEG_BIG)
        s_buf[h, :kr, cols] = st
        if kr < tk:
            s_buf[h, kr:, cols] = jnp.full((tk - kr, cw), NEG_BIG, F32)
        tm_buf[h:h + 1, cols] = jnp.max(st, axis=0, keepdims=True)

    def value_chunk(vblk, dj, s_buf, tm_buf, h, n, key_start):
        idx = 2 * h + (n * cw) // tq
        cols = slice(n * cw, (n + 1) * cw)
        qcols = slice(q_start(n), q_start(n) + cw)
        kr = live_rows(n, key_start)
        off = slope_tk[h][:, qcols] * dj
        vaug = jnp.concatenate([vblk[DIFF_V * h:DIFF_V * (h + 1)], ones_rows], axis=0)
        m_old = m_s[idx:idx + 1, qcols]
        m_new = jnp.maximum(m_old, tm_buf[h:h + 1, cols] + off)
        alpha = jnp.exp2(m_old - m_new)
        p = jnp.exp2(s_buf[h, :kr, cols] - (m_new - off)).astype(BF16)
        pv = jnp.dot(vaug[:, :kr], p, preferred_element_type=F32)
        acc_s[idx, :, qcols] = alpha * acc_s[idx, :, qcols] + pv
        m_s[idx:idx + 1, qcols] = m_new

    def key_lhs(blk):
        kblk = k_ref[0, pl.ds(pl.multiple_of(blk * tk, tk), tk), :]
        own = lax.broadcasted_iota(jnp.int32, (tk, LANES), 1) < 2 * DIFF_QK
        return [jnp.where(own, kblk, bias_ref[0, 0]), jnp.where(own, bias_ref[0, 1], kblk)]

    def stage(score=None, value=None):
        s_chunks, v_chunks = [], []
        if score is not None:
            s_blk, s_bufs, s_start, s_chunks = score
            lhs = key_lhs(s_blk)
        if value is not None:
            v_blk, v_bufs, v_start, v_chunks = value
            vblk = vt_ref[0, v_blk]
            dj = jnp.full((1, cw), v_blk - first_diag, jnp.int32).astype(F32)
        for c in range(max(len(s_chunks), len(v_chunks))):
            if c < len(s_chunks):
                score_chunk(lhs, *s_bufs, *s_chunks[c], s_start)
            if c < len(v_chunks):
                value_chunk(vblk, dj, *v_bufs, *v_chunks[c], v_start)

    buf_a, buf_b = (sa_s, tma_s), (sb_s, tmb_s)
    first_diag = (tq // tk) * i
    head = [[c for c in chunks if c[0] == h] for h in range(2)]
    late = [[c for c in head[h] if q_start(c[1]) >= tk] for h in range(2)]
    stage(score=(first_diag + 1, buf_a, tk, late[0]))
    stage(score=(first_diag + 1, buf_a, tk, late[1]), value=(first_diag + 1, buf_a, tk, late[0]))
    stage(score=(first_diag, buf_b, 0, head[0]), value=(first_diag + 1, buf_a, tk, late[1]))
    stage(score=(first_diag, buf_b, 0, head[1]), value=(first_diag, buf_b, 0, head[0]))

    def pair(p, prev):
        stage(score=(2 * p, buf_a, None, head[0]), value=(prev, buf_b, None, head[1]))
        stage(score=(2 * p, buf_a, None, head[1]), value=(2 * p, buf_a, None, head[0]))
        stage(score=(2 * p + 1, buf_b, None, head[0]), value=(2 * p, buf_a, None, head[1]))
        stage(score=(2 * p + 1, buf_b, None, head[1]), value=(2 * p + 1, buf_b, None, head[0]))
        return 2 * p + 1

    tiles_needed = None
    for h in range(2):
        gap = None
        for mm in range(2):
            idx = 2 * h + mm
            m_now = jnp.maximum(m_s[idx:idx + 1, :], tmb_s[h:h + 1, mm * tq:(mm + 1) * tq])
            g = jnp.max(score_bound[idx] * BOUND_SLACK_REL + BOUND_SLACK_ABS - m_now, axis=1, keepdims=True)
            gap = g if gap is None else jnp.maximum(gap, g)
        need_h = (gap + ZERO_PROB_EXPONENT) / slope_ref[0, h:h + 1, 0:1] + (tk - 1) / tk
        tiles_needed = need_h if tiles_needed is None else jnp.maximum(tiles_needed, need_h)
    n_full = jnp.full((1, 1), first_diag, jnp.int32).astype(F32)
    first_tile = jnp.ceil(n_full - tiles_needed - 0.01)
    first_tile = jnp.where(first_tile == first_tile, jnp.clip(first_tile, 0.0, n_full), 0.0)
    first_tile = jnp.max(first_tile.astype(jnp.int32))
    starts_odd = first_tile % 2

    @pl.when(starts_odd == 1)
    def _():
        stage(score=(first_tile, buf_b, None, head[0]), value=(first_diag, buf_b, None, head[1]))
        stage(score=(first_tile, buf_b, None, head[1]), value=(first_tile, buf_b, None, head[0]))

    pending = jnp.where(starts_odd == 1, first_tile, first_diag)
    last = lax.fori_loop(first_tile // 2 + starts_odd, i, pair, pending)
    stage(value=(last, buf_b, None, head[1]))

    lamv = lam_ref[...]
    lam = (jnp.exp(jnp.sum(lamv[0:1] * lamv[1:2], axis=1, keepdims=True))
           - jnp.exp(jnp.sum(lamv[2:3] * lamv[3:4], axis=1, keepdims=True)) + LAM_INIT)
    gain = jnp.concatenate([g_ref[...]] * (tq // LANES), axis=1)
    ys = []
    for h in range(2):
        a0 = acc_s[2 * h]
        a1 = acc_s[2 * h + 1]
        a = a0[0:DIFF_V] / a0[DIFF_V:DIFF_V + 1] - lam * (a1[0:DIFF_V] / a1[DIFF_V:DIFF_V + 1])
        ms = jnp.mean(a * a, axis=0, keepdims=True)
        ys.append(a * lax.rsqrt(ms + RMS_EPS) * gain * (1.0 - LAM_INIT))
    o_ref[0] = (jnp.concatenate(ys, axis=0).T * sz_ref[0].astype(F32)).astype(BF16)


def _attn(q, k, vt, kabs, sz, bias, slope_tk, lam_vecs, gain):
    b, s, _ = q.shape
    tq, tk = Q_TILE, K_TILE
    pairs = DIFF_HEADS // 2
    return pl.pallas_call(
        _attn_kernel,
        out_shape=jax.ShapeDtypeStruct((b, s, BRANCH), BF16),
        grid=(b, pairs, s // tq),
        in_specs=[pl.BlockSpec((1, tq, LANES), lambda bi, hp, qi: (bi, qi, hp)),
                  pl.BlockSpec((1, s, LANES), lambda bi, hp, qi: (bi, 0, hp)),
                  pl.BlockSpec((1, s // tk, LANES, tk), lambda bi, hp, qi: (bi, 0, hp, 0)),
                  pl.BlockSpec((1, s // tk, 8, LANES), lambda bi, hp, qi: (bi, 0, 0, hp)),
                  pl.BlockSpec((1, tq, LANES), lambda bi, hp, qi: (bi, qi, hp)),
                  pl.BlockSpec((1, 2, tk, LANES), lambda bi, hp, qi: (hp, 0, 0, 0)),
                  pl.BlockSpec((1, 8, LANES), lambda bi, hp, qi: (hp, 0, 0)),
                  _const_spec((8, LANES), single_buffer=False),
                  _const_spec((DIFF_V, LANES), single_buffer=False)],
        out_specs=pl.BlockSpec((1, tq, LANES), lambda bi, hp, qi: (bi, qi, hp)),
        scratch_shapes=[pltpu.VMEM((2, LANES, 2 * tq), BF16),
                        pltpu.VMEM((8, tq), F32),
                        pltpu.VMEM((4, V_AUG, tq), F32),
                        pltpu.VMEM((2, tk, 2 * tq), F32),
                        pltpu.VMEM((2, tk, 2 * tq), F32),
                        pltpu.VMEM((8, 2 * tq), F32),
                        pltpu.VMEM((8, 2 * tq), F32)],
        compiler_params=pltpu.CompilerParams(dimension_semantics=("parallel", "parallel", "arbitrary"),
                                             vmem_limit_bytes=VMEM_LIMIT),
        name="attn",
    )(q, k, vt, kabs, sz, bias, slope_tk, lam_vecs, gain)


def _epi_kernel(x_ref, odiff_ref, part_ref, gd_ref, lng_ref, lnb_ref, wb1_ref, wout_ref, og_ref, ob_ref, o_ref):
    h = _layer_norm(x_ref[0], lng_ref[...], lnb_ref[...])
    y_diff = jnp.dot(odiff_ref[0], wb1_ref[...], preferred_element_type=F32)
    merged = part_ref[0].astype(F32) + gd_ref[0].astype(F32) * y_diff
    out = jnp.dot(merged.astype(BF16), wout_ref[...], preferred_element_type=F32)
    o_ref[0] = _layer_norm(DEEPNORM_ALPHA * h + out, og_ref[...], ob_ref[...])


def _epi(x, o_diff, part, gd, ln_g, ln_b, wb1, w_out, out_g, out_b):
    b, s, _ = x.shape
    tt = TOKEN_TILE
    tok = lambda width: pl.BlockSpec((1, tt, width), lambda bi, si: (bi, si, 0))
    return pl.pallas_call(
        _epi_kernel,
        out_shape=jax.ShapeDtypeStruct((b, s, D_MODEL), F32),
        grid=(b, s // tt),
        in_specs=[tok(D_MODEL), tok(BRANCH), tok(D_MODEL), tok(D_MODEL),
                  _const_spec((1, D_MODEL)), _const_spec((1, D_MODEL)),
                  _const_spec((BRANCH, D_MODEL)), _const_spec((D_MODEL, D_MODEL)),
                  _const_spec((1, D_MODEL)), _const_spec((1, D_MODEL))],
        out_specs=tok(D_MODEL),
        compiler_params=pltpu.CompilerParams(dimension_semantics=("parallel", "parallel"),
                                             vmem_limit_bytes=VMEM_LIMIT),
        name="epi",
    )(x, o_diff, part, gd, ln_g, ln_b, wb1, w_out, out_g, out_b)


def _alibi_tables():
    pos = np.arange(K_TILE)
    lo = (pos % 256).astype(np.float32)
    hi = (pos - pos % 256).astype(np.float32)
    slopes = [2.0 ** (-8.0 * (h + 1) / DIFF_HEADS) for h in range(DIFF_HEADS)]
    bias = np.zeros((DIFF_HEADS // 2, 2, K_TILE, LANES), np.float32)
    slope_tk = np.zeros((DIFF_HEADS // 2, 8, LANES), np.float32)
    for hp in range(DIFF_HEADS // 2):
        for h in range(2):
            sl = slopes[2 * hp + h]
            for t in range(len(LOG2E_TERMS)):
                bias[hp, h, :, _alibi_lane(h) + 2 * t] = sl * lo
                bias[hp, h, :, _alibi_lane(h) + 2 * t + 1] = sl * hi
            slope_tk[hp, h, :] = sl * K_TILE * LOG2E
    return jnp.asarray(bias, BF16), jnp.asarray(slope_tk, F32)


def kernel(x, mem, ln_in_g, ln_in_b, w_in, b_gate, pool_w, pool_scale, lambda_q1, lambda_k1, lambda_q2, lambda_k2,
           diff_norm_g, w_mem_kv, w_branch, w_out, ln_out_g, ln_out_b):
    assert w_in.shape[0] == DEPTH == 1
    assert x.shape[2] == D_MODEL and x.shape[1] % Q_TILE == 0 and mem.shape[1:] == (MEM_LEN, D_MODEL)
    assert Q_TILE == 2 * K_TILE and K_TILE % CHUNK == 0 and K_TILE <= 512
    w = w_in[0]
    seg = lambda i: w[:, i * BRANCH:(i + 1) * BRANCH]
    w_main = jnp.concatenate([seg(0), seg(1), seg(2), seg(3), seg(5), seg(6), seg(7)], axis=1).astype(BF16)
    w_vt = seg(4).T.astype(BF16)
    w_g = w[:, 8 * BRANCH:].astype(BF16)
    row = lambda v: v.reshape(1, -1).astype(F32)

    mk, mv = _memkv(mem, w_mem_kv[0].astype(BF16))
    q, k, vt, sz, part, gd, kabs = _proj(
        x, row(ln_in_g), row(ln_in_b), w_main, w_vt, w_g, row(b_gate[0]), pool_w[0].astype(BF16),
        row(pool_scale[0]), mk, mv, w_branch[0, 0].astype(BF16), w_branch[0, 2].astype(BF16))

    bias, slope_tk = _alibi_tables()
    lam_vecs = jnp.concatenate([lambda_q1, lambda_k1, lambda_q2, lambda_k2], axis=0).astype(F32)
    lam_vecs = jnp.pad(lam_vecs, ((0, 4), (0, LANES - DIFF_QK)))
    gain = jnp.broadcast_to(diff_norm_g[0].astype(F32)[:, None], (DIFF_V, LANES))
    o_diff = _attn(q, k, vt, kabs, sz, bias, slope_tk, lam_vecs, gain)

    return _epi(x, o_diff, part, gd, row(ln_in_g), row(ln_in_b), w_branch[0, 1].astype(BF16),
                w_out[0].astype(BF16), row(ln_out_g[0]), row(ln_out_b[0]))
```

```python
import math
import struct

import jax
import jax.numpy as jnp
import numpy as np
from jax import lax
from jax.experimental import pallas as pl
from jax.experimental.pallas import tpu as pltpu

F32 = jnp.float32
BF16 = jnp.bfloat16

D_MODEL = 1024
BRANCH = 512
MEM_LEN = 256
POOL_WINDOWS = (2, 4, 8, 16)
POOL_GROUP = 128
DIFF_HEADS = 8
DIFF_QK = 32
DIFF_V = 64
MEM_HEADS = 4
MEM_HEAD_DIM = 128
LN_EPS = 1e-5
RMS_EPS = 1e-5
DEPTH = 1
DEEPNORM_ALPHA = (2.0 * DEPTH) ** 0.25
LAM_INIT = 0.8 - 0.6 * math.exp(-0.3 * 0)
LOG2E = math.log2(math.e)
DIFF_SCALE = DIFF_QK ** -0.5 * LOG2E


def _bf16_terms(x, n):
    terms = []
    for _ in range(n):
        bits = struct.unpack("<I", struct.pack("<f", x))[0]
        bits = (bits + 0x7FFF + ((bits >> 16) & 1)) & 0xFFFF0000
        t = struct.unpack("<f", struct.pack("<I", bits))[0]
        terms.append(t)
        x -= t
    return terms


LOG2E_TERMS = _bf16_terms(LOG2E, 3)
MEM_SCALE = MEM_HEAD_DIM ** -0.5
NEG_BIG = -1e30
ZERO_PROB_EXPONENT = 151.0
BOUND_SLACK_REL = 1.001
BOUND_SLACK_ABS = 1.0
NORM_SLACK_REL = 1.01

LANES = 128
BF16_SUBLANES = 16
VMEM_BYTES_V7X = 64 * 1024 * 1024
VMEM_LIMIT = VMEM_BYTES_V7X - 8 * 1024 * 1024

TOKEN_TILE = 512
K_TILE = TOKEN_TILE
Q_TILE = 2 * K_TILE
CHUNK = 256
HALO = 32
V_AUG = DIFF_V + BF16_SUBLANES
ALIBI_COLS = 2 * len(LOG2E_TERMS)
NT_DIMS = (((1,), (1,)), ((), ()))


def _alibi_lane(h):
    return 2 * DIFF_QK * (1 - h)


def _layer_norm(x, g, b):
    mu = jnp.mean(x, axis=-1, keepdims=True)
    xc = x - mu
    var = jnp.mean(xc * xc, axis=-1, keepdims=True)
    return xc * lax.rsqrt(var + LN_EPS) * g + b


def _silu(x):
    return x * jax.nn.sigmoid(x)


def _const_spec(shape, single_buffer=True):
    n = len(shape)
    mode = pl.Buffered(1) if single_buffer else None
    return pl.BlockSpec(shape, lambda *_: (0,) * n, pipeline_mode=mode)


def _memkv_kernel(mem_ref, w_ref, mk_ref, mv_ref):
    kv = jnp.dot(mem_ref[0].astype(BF16), w_ref[...], preferred_element_type=F32)
    mk_ref[0] = kv[:, :BRANCH].astype(BF16)
    mv_ref[0] = kv[:, BRANCH:].astype(BF16)


def _memkv(mem, w_kv):
    b = mem.shape[0]
    return pl.pallas_call(
        _memkv_kernel,
        out_shape=(jax.ShapeDtypeStruct((b, MEM_LEN, BRANCH), BF16),
                   jax.ShapeDtypeStruct((b, MEM_LEN, BRANCH), BF16)),
        grid=(b,),
        in_specs=[pl.BlockSpec((1, MEM_LEN, D_MODEL), lambda i: (i, 0, 0)),
                  _const_spec((D_MODEL, 2 * BRANCH))],
        out_specs=(pl.BlockSpec((1, MEM_LEN, BRANCH), lambda i: (i, 0, 0)),
                   pl.BlockSpec((1, MEM_LEN, BRANCH), lambda i: (i, 0, 0))),
        name="memkv",
    )(mem, w_kv)


def _proj_kernel(x_ref, lng_ref, lnb_ref, wmain_ref, wvt_ref, wg_ref, bg_ref, poolw_ref, pscale_ref,
                 mk_ref, mv_ref, wb0_ref, wb2_ref, segsum_ref,
                 q_ref, k_ref, vt_ref, sz_ref, part_ref, gd_ref, kabs_ref,
                 e1_ref, e2_ref, e4_ref, e8_ref):
    tt = TOKEN_TILE
    s_idx = pl.program_id(1)
    h = _layer_norm(x_ref[0], lng_ref[...], lnb_ref[...])
    hb = h.astype(BF16)

    def proj(seg):
        return jnp.dot(hb, wmain_ref[:, seg * BRANCH:(seg + 1) * BRANCH], preferred_element_type=F32)

    u = proj(0)

    @pl.when(s_idx == 0)
    def _():
        e1_ref[0:HALO, :] = jnp.zeros((HALO, BRANCH), F32)

    e1_ref[HALO:HALO + tt, :] = u
    n = HALO + tt
    e2_ref[8:n, :] = e1_ref[8:n, :] + e1_ref[7:n - 1, :]
    e4_ref[16:n, :] = e2_ref[16:n, POOL_GROUP:] + e2_ref[14:n - 2, POOL_GROUP:]
    e8_ref[24:n, :] = e4_ref[24:n, POOL_GROUP:] + e4_ref[20:n - 4, POOL_GROUP:]
    s16 = e8_ref[HALO:n, POOL_GROUP:] + e8_ref[HALO - 8:n - 8, POOL_GROUP:]
    wsum = (e2_ref[HALO:n, 0:POOL_GROUP], e4_ref[HALO:n, 0:POOL_GROUP], e8_ref[HALO:n, 0:POOL_GROUP], s16)
    e1_ref[0:HALO, :] = e1_ref[tt:tt + HALO, :]

    t_pos = s_idx * tt + lax.broadcasted_iota(jnp.int32, (tt, POOL_GROUP), 0)
    pooled = []
    for g, w in enumerate(POOL_WINDOWS):
        cnt = jnp.minimum(t_pos + 1, w).astype(F32)
        pg = wsum[g] / cnt - u[:, g * POOL_GROUP:(g + 1) * POOL_GROUP]
        pooled.append(jnp.dot(pg.astype(BF16), poolw_ref[g], preferred_element_type=F32))
    o_pool = jnp.concatenate(pooled, axis=1) * pscale_ref[...] * _silu(proj(1))

    mq = (proj(5) * MEM_SCALE).astype(BF16)
    heads = []
    for hh in range(MEM_HEADS):
        sl = slice(hh * MEM_HEAD_DIM, (hh + 1) * MEM_HEAD_DIM)
        s = lax.dot_general(mq[:, sl], mk_ref[0, :, sl], NT_DIMS, preferred_element_type=F32)
        p = jnp.exp(s - jnp.max(s, axis=-1, keepdims=True))
        l = jnp.sum(p, axis=-1, keepdims=True)
        heads.append(jnp.dot(p.astype(BF16), mv_ref[0, :, sl], preferred_element_type=F32) / l)
    o_mem = jnp.concatenate(heads, axis=1) * _silu(proj(6))

    def gate(nb):
        sl = slice(nb * D_MODEL, (nb + 1) * D_MODEL)
        return jax.nn.sigmoid(jnp.dot(hb, wg_ref[:, sl], preferred_element_type=F32) + bg_ref[:, sl])

    y_pool = jnp.dot(o_pool.astype(BF16), wb0_ref[...], preferred_element_type=F32)
    y_mem = jnp.dot(o_mem.astype(BF16), wb2_ref[...], preferred_element_type=F32)
    part_ref[0] = (gate(0) * y_pool + gate(2) * y_mem).astype(BF16)
    gd_ref[0] = gate(1).astype(BF16)

    q_ref[0] = (proj(2) * DIFF_SCALE).astype(BF16)
    kb = proj(3).astype(BF16)
    k_ref[0] = kb
    kf = kb.astype(F32)
    norm2 = jnp.dot((kf * kf).astype(BF16), segsum_ref[...], preferred_element_type=F32)
    kabs_ref[0, 0] = jnp.concatenate(
        [jnp.max(jnp.abs(kf), axis=0, keepdims=True),
         jnp.concatenate([jnp.max(norm2, axis=0, keepdims=True)] * (BRANCH // LANES), axis=1),
         jnp.zeros((6, BRANCH), F32)], axis=0)
    vt = lax.dot_general(wvt_ref[...], hb, NT_DIMS, preferred_element_type=F32)
    vt_ref[0, 0] = vt.astype(BF16)
    sz_ref[0] = _silu(proj(4)).astype(BF16)


def _proj(x, ln_g, ln_b, w_main, w_vt, w_g, b_g, pool_w, pool_scale, mk, mv, wb0, wb2):
    b, s, _ = x.shape
    tt = TOKEN_TILE
    ns = s // tt
    tok = lambda width: pl.BlockSpec((1, tt, width), lambda bi, si: (bi, si, 0))
    per_batch = pl.BlockSpec((1, MEM_LEN, BRANCH), lambda bi, si: (bi, 0, 0))
    segsum = jnp.asarray(np.arange(BRANCH)[:, None] // DIFF_QK == np.arange(LANES)[None, :], BF16)
    return pl.pallas_call(
        _proj_kernel,
        out_shape=(jax.ShapeDtypeStruct((b, s, BRANCH), BF16),
                   jax.ShapeDtypeStruct((b, s, BRANCH), BF16),
                   jax.ShapeDtypeStruct((b, ns, BRANCH, tt), BF16),
                   jax.ShapeDtypeStruct((b, s, BRANCH), BF16),
                   jax.ShapeDtypeStruct((b, s, D_MODEL), BF16),
                   jax.ShapeDtypeStruct((b, s, D_MODEL), BF16),
                   jax.ShapeDtypeStruct((b, ns, 8, BRANCH), F32)),
        grid=(b, ns),
        in_specs=[tok(D_MODEL),
                  _const_spec((1, D_MODEL)), _const_spec((1, D_MODEL)),
                  _const_spec((D_MODEL, 7 * BRANCH)), _const_spec((BRANCH, D_MODEL)),
                  _const_spec((D_MODEL, 3 * D_MODEL)), _const_spec((1, 3 * D_MODEL)),
                  _const_spec((len(POOL_WINDOWS), POOL_GROUP, POOL_GROUP)), _const_spec((1, BRANCH)),
                  per_batch, per_batch,
                  _const_spec((BRANCH, D_MODEL)), _const_spec((BRANCH, D_MODEL)),
                  _const_spec((BRANCH, LANES))],
        out_specs=(tok(BRANCH), tok(BRANCH),
                   pl.BlockSpec((1, 1, BRANCH, tt), lambda bi, si: (bi, si, 0, 0)),
                   tok(BRANCH), tok(D_MODEL), tok(D_MODEL),
                   pl.BlockSpec((1, 1, 8, BRANCH), lambda bi, si: (bi, si, 0, 0))),
        scratch_shapes=[pltpu.VMEM((HALO + tt, BRANCH), F32),
                        pltpu.VMEM((HALO + tt, BRANCH), F32),
                        pltpu.VMEM((HALO + tt, BRANCH - POOL_GROUP), F32),
                        pltpu.VMEM((HALO + tt, BRANCH - 2 * POOL_GROUP), F32)],
        compiler_params=pltpu.CompilerParams(dimension_semantics=("arbitrary", "arbitrary"),
                                             vmem_limit_bytes=VMEM_LIMIT),
        name="proj",
    )(x, ln_g, ln_b, w_main, w_vt, w_g, b_g, pool_w, pool_scale, mk, mv, wb0, wb2, segsum)


def _attn_kernel(q_ref, k_ref, vt_ref, kabs_ref, sz_ref, bias_ref, slope_ref, lam_ref, g_ref, o_ref,
                 qt_s, m_s, acc_s,
                 sa_s, sb_s, tma_s, tmb_s):
    tq, tk = Q_TILE, K_TILE
    i = pl.program_id(2)

    qt32 = q_ref[0].astype(F32).T
    qt = qt32.astype(BF16)
    kstat = jnp.max(kabs_ref[0], axis=0)
    kcol = kstat.T[:, 0:1]
    qk_bound = jnp.abs(qt32) * kcol
    q_sq = qt32 * qt32
    lane = lax.broadcasted_iota(jnp.int32, (1, LANES), 1)
    score_bound = []
    for idx in range(4):
        rows = slice(DIFF_QK * idx, DIFF_QK * (idx + 1))
        k_sq = jnp.max(jnp.where(lane == 4 * pl.program_id(1) + idx, kstat[1:2], 0.0), axis=1, keepdims=True)
        by_norm = jnp.sqrt(jnp.sum(q_sq[rows], axis=0, keepdims=True) * (k_sq * NORM_SLACK_REL))
        score_bound.append(jnp.minimum(jnp.sum(qk_bound[rows], axis=0, keepdims=True), by_norm))

    @pl.when(i == 0)
    def _():
        row = lax.broadcasted_iota(jnp.int32, (BF16_SUBLANES, 2 * tq), 0)
        for h in range(2):
            qt_s[h] = jnp.zeros((LANES, 2 * tq), BF16)
            alibi_rows = jnp.zeros((BF16_SUBLANES, 2 * tq), F32)
            for t, c in enumerate(LOG2E_TERMS):
                alibi_rows = jnp.where((row == 2 * t) | (row == 2 * t + 1), c, alibi_rows)
            qt_s[h, _alibi_lane(h):_alibi_lane(h) + BF16_SUBLANES, :] = alibi_rows.astype(BF16)

    for h in range(2):
        base = 2 * DIFF_QK * h
        qt_s[h, base:base + DIFF_QK, 0:tq] = qt[base:base + DIFF_QK]
        qt_s[h, base + DIFF_QK:base + 2 * DIFF_QK, tq:2 * tq] = qt[base + DIFF_QK:base + 2 * DIFF_QK]
    m_s[...] = jnp.full(m_s.shape, NEG_BIG, F32)
    acc_s[...] = jnp.zeros(acc_s.shape, F32)

    ones_rows = jnp.where(lax.broadcasted_iota(jnp.int32, (BF16_SUBLANES, tk), 0) == 0, 1.0, 0.0).astype(BF16)
    slope_tk = [jnp.concatenate([slope_ref[0, h:h + 1, :]] * (tq // LANES), axis=1) for h in range(2)]

    cw = CHUNK
    chunks = [(h, n) for h in range(2) for n in range(2 * tq // cw)]

    def q_start(n):
        return (n * cw) % tq

    def live_rows(n, key_start):
        return tk if key_start is None else min(tk, q_start(n) + cw - key_start)

    def score_chunk(lhs, s_buf, tm_buf, h, n, key_start):
        cols = slice(n * cw, (n + 1) * cw)
        kr = live_rows(n, key_start)
        st = jnp.dot(lhs[h][:kr], qt_s[h, :, cols], preferred_element_type=F32)
        if key_start is not None and key_start + kr - 1 > q_start(n):
            kk = lax.broadcasted_iota(jnp.int32, (kr, cw), 0) + key_start
            qq = lax.broadcasted_iota(jnp.int32, (kr, cw), 1) + q_start(n)
            st = jnp.where(kk <= qq, st, NEG_BIG)
        s_buf[h, :kr, cols] = st
        if kr < tk:
            s_buf[h, kr:, cols] = jnp.full((tk - kr, cw), NEG_BIG, F32)
        tm_buf[h:h + 1, cols] = jnp.max(st, axis=0, keepdims=True)

    def value_chunk(vblk, dj, s_buf, tm_buf, h, n, key_start):
        idx = 2 * h + (n * cw) // tq
        cols = slice(n * cw, (n + 1) * cw)
        qcols = slice(q_start(n), q_start(n) + cw)
        kr = live_rows(n, key_start)
        off = slope_tk[h][:, qcols] * dj
        vaug = jnp.concatenate([vblk[DIFF_V * h:DIFF_V * (h + 1)], ones_rows], axis=0)
        m_old = m_s[idx:idx + 1, qcols]
        m_new = jnp.maximum(m_old, tm_buf[h:h + 1, cols] + off)
        alpha = jnp.exp2(m_old - m_new)
        p = jnp.exp2(s_buf[h, :kr, cols] - (m_new - off)).astype(BF16)
        pv = jnp.dot(vaug[:, :kr], p, preferred_element_type=F32)
        acc_s[idx, :, qcols] = alpha * acc_s[idx, :, qcols] + pv
        m_s[idx:idx + 1, qcols] = m_new

    def key_lhs(blk):
        kblk = k_ref[0, pl.ds(pl.multiple_of(blk * tk, tk), tk), :]
        own = lax.broadcasted_iota(jnp.int32, (tk, LANES), 1) < 2 * DIFF_QK
        return [jnp.where(own, kblk, bias_ref[0, 0]), jnp.where(own, bias_ref[0, 1], kblk)]

    def stage(score=None, value=None):
        s_chunks, v_chunks = [], []
        if score is not None:
            s_blk, s_bufs, s_start, s_chunks = score
            lhs = key_lhs(s_blk)
        if value is not None:
            v_blk, v_bufs, v_start, v_chunks = value
            vblk = vt_ref[0, v_blk]
            dj = jnp.full((1, cw), v_blk - first_diag, jnp.int32).astype(F32)
        for c in range(max(len(s_chunks), len(v_chunks))):
            if c < len(s_chunks):
                score_chunk(lhs, *s_bufs, *s_chunks[c], s_start)
            if c < len(v_chunks):
                value_chunk(vblk, dj, *v_bufs, *v_chunks[c], v_start)

    buf_a, buf_b = (sa_s, tma_s), (sb_s, tmb_s)
    first_diag = (tq // tk) * i
    head = [[c for c in chunks if c[0] == h] for h in range(2)]
    late = [[c for c in head[h] if q_start(c[1]) >= tk] for h in range(2)]
    stage(score=(first_diag + 1, buf_a, tk, late[0]))
    stage(score=(first_diag + 1, buf_a, tk, late[1]), value=(first_diag + 1, buf_a, tk, late[0]))
    stage(score=(first_diag, buf_b, 0, head[0]), value=(first_diag + 1, buf_a, tk, late[1]))
    stage(score=(first_diag, buf_b, 0, head[1]), value=(first_diag, buf_b, 0, head[0]))

    def pair(p, prev):
        stage(score=(2 * p, buf_a, None, head[0]), value=(prev, buf_b, None, head[1]))
        stage(score=(2 * p, buf_a, None, head[1]), value=(2 * p, buf_a, None, head[0]))
        stage(score=(2 * p + 1, buf_b, None, head[0]), value=(2 * p, buf_a, None, head[1]))
        stage(score=(2 * p + 1, buf_b, None, head[1]), value=(2 * p + 1, buf_b, None, head[0]))
        return 2 * p + 1

    tiles_needed = None
    for h in range(2):
        gap = None
        for mm in range(2):
            idx = 2 * h + mm
            m_now = jnp.maximum(m_s[idx:idx + 1, :], tmb_s[h:h + 1, mm * tq:(mm + 1) * tq])
            g = jnp.max(score_bound[idx] * BOUND_SLACK_REL + BOUND_SLACK_ABS - m_now, axis=1, keepdims=True)
            gap = g if gap is None else jnp.maximum(gap, g)
        need_h = (gap + ZERO_PROB_EXPONENT) / slope_ref[0, h:h + 1, 0:1] + (tk - 1) / tk
        tiles_needed = need_h if tiles_needed is None else jnp.maximum(tiles_needed, need_h)
    n_full = jnp.full((1, 1), first_diag, jnp.int32).astype(F32)
    first_tile = jnp.ceil(n_full - tiles_needed - 0.01)
    first_tile = jnp.where(first_tile == first_tile, jnp.clip(first_tile, 0.0, n_full), 0.0)
    first_tile = jnp.max(first_tile.astype(jnp.int32))
    starts_odd = first_tile % 2

    @pl.when(starts_odd == 1)
    def _():
        stage(score=(first_tile, buf_b, None, head[0]), value=(first_diag, buf_b, None, head[1]))
        stage(score=(first_tile, buf_b, None, head[1]), value=(first_tile, buf_b, None, head[0]))

    pending = jnp.where(starts_odd == 1, first_tile, first_diag)
    first_pair = first_tile // 2 + starts_odd
    n_double = (i - first_pair) // 2

    def double_pair(t, prev):
        return pair(first_pair + 2 * t + 1, pair(first_pair + 2 * t, prev))

    last = lax.fori_loop(0, n_double, double_pair, pending)

    @pl.when((i - first_pair) % 2 == 1)
    def _():
        pair(i - 1, last)

    last = jnp.where(i > first_pair, 2 * i - 1, pending)
    stage(value=(last, buf_b, None, head[1]))

    lamv = lam_ref[...]
    lam = (jnp.exp(jnp.sum(lamv[0:1] * lamv[1:2], axis=1, keepdims=True))
           - jnp.exp(jnp.sum(lamv[2:3] * lamv[3:4], axis=1, keepdims=True)) + LAM_INIT)
    gain = jnp.concatenate([g_ref[...]] * (tq // LANES), axis=1)
    ys = []
    for h in range(2):
        a0 = acc_s[2 * h]
        a1 = acc_s[2 * h + 1]
        a = a0[0:DIFF_V] / a0[DIFF_V:DIFF_V + 1] - lam * (a1[0:DIFF_V] / a1[DIFF_V:DIFF_V + 1])
        ms = jnp.mean(a * a, axis=0, keepdims=True)
        ys.append(a * lax.rsqrt(ms + RMS_EPS) * gain * (1.0 - LAM_INIT))
    o_ref[0] = (jnp.concatenate(ys, axis=0).T * sz_ref[0].astype(F32)).astype(BF16)


def _attn(q, k, vt, kabs, sz, bias, slope_tk, lam_vecs, gain):
    b, s, _ = q.shape
    tq, tk = Q_TILE, K_TILE
    pairs = DIFF_HEADS // 2
    return pl.pallas_call(
        _attn_kernel,
        out_shape=jax.ShapeDtypeStruct((b, s, BRANCH), BF16),
        grid=(b, pairs, s // tq),
        in_specs=[pl.BlockSpec((1, tq, LANES), lambda bi, hp, qi: (bi, qi, hp)),
                  pl.BlockSpec((1, s, LANES), lambda bi, hp, qi: (bi, 0, hp)),
                  pl.BlockSpec((1, s // tk, LANES, tk), lambda bi, hp, qi: (bi, 0, hp, 0)),
                  pl.BlockSpec((1, s // tk, 8, LANES), lambda bi, hp, qi: (bi, 0, 0, hp)),
                  pl.BlockSpec((1, tq, LANES), lambda bi, hp, qi: (bi, qi, hp)),
                  pl.BlockSpec((1, 2, tk, LANES), lambda bi, hp, qi: (hp, 0, 0, 0)),
                  pl.BlockSpec((1, 8, LANES), lambda bi, hp, qi: (hp, 0, 0)),
                  _const_spec((8, LANES), single_buffer=False),
                  _const_spec((DIFF_V, LANES), single_buffer=False)],
        out_specs=pl.BlockSpec((1, tq, LANES), lambda bi, hp, qi: (bi, qi, hp)),
        scratch_shapes=[pltpu.VMEM((2, LANES, 2 * tq), BF16),
                        pltpu.VMEM((8, tq), F32),
                        pltpu.VMEM((4, V_AUG, tq), F32),
                        pltpu.VMEM((2, tk, 2 * tq), F32),
                        pltpu.VMEM((2, tk, 2 * tq), F32),
                        pltpu.VMEM((8, 2 * tq), F32),
                        pltpu.VMEM((8, 2 * tq), F32)],
        compiler_params=pltpu.CompilerParams(dimension_semantics=("parallel", "parallel", "arbitrary"),
                                             vmem_limit_bytes=VMEM_LIMIT),
        name="attn",
    )(q, k, vt, kabs, sz, bias, slope_tk, lam_vecs, gain)


def _epi_kernel(x_ref, odiff_ref, part_ref, gd_ref, lng_ref, lnb_ref, wb1_ref, wout_ref, og_ref, ob_ref, o_ref):
    h = _layer_norm(x_ref[0], lng_ref[...], lnb_ref[...])
    y_diff = jnp.dot(odiff_ref[0], wb1_ref[...], preferred_element_type=F32)
    merged = part_ref[0].astype(F32) + gd_ref[0].astype(F32) * y_diff
    out = jnp.dot(merged.astype(BF16), wout_ref[...], preferred_element_type=F32)
    o_ref[0] = _layer_norm(DEEPNORM_ALPHA * h + out, og_ref[...], ob_ref[...])


def _epi(x, o_diff, part, gd, ln_g, ln_b, wb1, w_out, out_g, out_b):
    b, s, _ = x.shape
    tt = TOKEN_TILE
    tok = lambda width: pl.BlockSpec((1, tt, width), lambda bi, si: (bi, si, 0))
    return pl.pallas_call(
        _epi_kernel,
        out_shape=jax.ShapeDtypeStruct((b, s, D_MODEL), F32),
        grid=(b, s // tt),
        in_specs=[tok(D_MODEL), tok(BRANCH), tok(D_MODEL), tok(D_MODEL),
                  _const_spec((1, D_MODEL)), _const_spec((1, D_MODEL)),
                  _const_spec((BRANCH, D_MODEL)), _const_spec((D_MODEL, D_MODEL)),
                  _const_spec((1, D_MODEL)), _const_spec((1, D_MODEL))],
        out_specs=tok(D_MODEL),
        compiler_params=pltpu.CompilerParams(dimension_semantics=("parallel", "parallel"),
                                             vmem_limit_bytes=VMEM_LIMIT),
        name="epi",
    )(x, o_diff, part, gd, ln_g, ln_b, wb1, w_out, out_g, out_b)


def _alibi_tables():
    pos = np.arange(K_TILE)
    lo = (pos % 256).astype(np.float32)
    hi = (pos - pos % 256).astype(np.float32)
    slopes = [2.0 ** (-8.0 * (h + 1) / DIFF_HEADS) for h in range(DIFF_HEADS)]
    bias = np.zeros((DIFF_HEADS // 2, 2, K_TILE, LANES), np.float32)
    slope_tk = np.zeros((DIFF_HEADS // 2, 8, LANES), np.float32)
    for hp in range(DIFF_HEADS // 2):
        for h in range(2):
            sl = slopes[2 * hp + h]
            for t in range(len(LOG2E_TERMS)):
                bias[hp, h, :, _alibi_lane(h) + 2 * t] = sl * lo
                bias[hp, h, :, _alibi_lane(h) + 2 * t + 1] = sl * hi
            slope_tk[hp, h, :] = sl * K_TILE * LOG2E
    return jnp.asarray(bias, BF16), jnp.asarray(slope_tk, F32)


def kernel(x, mem, ln_in_g, ln_in_b, w_in, b_gate, pool_w, pool_scale, lambda_q1, lambda_k1, lambda_q2, lambda_k2,
           diff_norm_g, w_mem_kv, w_branch, w_out, ln_out_g, ln_out_b):
    assert w_in.shape[0] == DEPTH == 1
    assert x.shape[2] == D_MODEL and x.shape[1] % Q_TILE == 0 and mem.shape[1:] == (MEM_LEN, D_MODEL)
    assert Q_TILE == 2 * K_TILE and K_TILE % CHUNK == 0 and K_TILE <= 512
    w = w_in[0]
    seg = lambda i: w[:, i * BRANCH:(i + 1) * BRANCH]
    w_main = jnp.concatenate([seg(0), seg(1), seg(2), seg(3), seg(5), seg(6), seg(7)], axis=1).astype(BF16)
    w_vt = seg(4).T.astype(BF16)
    w_g = w[:, 8 * BRANCH:].astype(BF16)
    row = lambda v: v.reshape(1, -1).astype(F32)

    mk, mv = _memkv(mem, w_mem_kv[0].astype(BF16))
    q, k, vt, sz, part, gd, kabs = _proj(
        x, row(ln_in_g), row(ln_in_b), w_main, w_vt, w_g, row(b_gate[0]), pool_w[0].astype(BF16),
        row(pool_scale[0]), mk, mv, w_branch[0, 0].astype(BF16), w_branch[0, 2].astype(BF16))

    bias, slope_tk = _alibi_tables()
    lam_vecs = jnp.concatenate([lambda_q1, lambda_k1, lambda_q2, lambda_k2], axis=0).astype(F32)
    lam_vecs = jnp.pad(lam_vecs, ((0, 4), (0, LANES - DIFF_QK)))
    gain = jnp.broadcast_to(diff_norm_g[0].astype(F32)[:, None], (DIFF_V, LANES))
    o_diff = _attn(q, k, vt, kabs, sz, bias, slope_tk, lam_vecs, gain)

    return _epi(x, o_diff, part, gd, row(ln_in_g), row(ln_in_b), w_branch[0, 1].astype(BF16),
                w_out[0].astype(BF16), row(ln_out_g[0]), row(ln_out_b[0]))
```
